```python
import jax
import jax.numpy as jnp
from jax import lax
import numpy as np

D_MODEL = 1024
BATCH = 8
SEQ = 2048
DEPTH = 4

GRID_W = 64
CTX_LEN = 256
NORM_EPS = 1e-6
N_MOD = 6

POOL_WINDOWS = (2, 4, 8, 16)
POOL_GROUP = 64
POOL_WIDTH = POOL_GROUP * len(POOL_WINDOWS)

N_HEADS = 8
N_KV_HEADS = 2
HEAD_DIM = 64
Q_PER_KV = N_HEADS // N_KV_HEADS
ATTN_WIDTH = N_HEADS * HEAD_DIM
KV_WIDTH = N_KV_HEADS * HEAD_DIM
Q_BLOCK = 128
ROPE_THETA = 10000.0

HG_HEADS = 4
HG_DK = 64
HG_DV = 64
HG_WIDTH = HG_HEADS * HG_DK
HG_CHUNK = 16

D_FF = ((8 * D_MODEL + 3 * 256 - 1) // (3 * 256)) * 256

IN_SIZES = (POOL_WIDTH,
            ATTN_WIDTH, KV_WIDTH, KV_WIDTH,
            HG_WIDTH, HG_WIDTH,
            HG_WIDTH, HG_WIDTH,
            HG_WIDTH,
            D_MODEL, D_MODEL, D_MODEL)
IN_WIDTH = sum(IN_SIZES)
IN_SPLITS = tuple(int(s) for s in np.cumsum(IN_SIZES)[:-1])

kernel_name = "hybrid_pool_gqa_hgrn2_dit_trunk"


def rmsnorm(x, w):
    xf = x.astype(jnp.float32)
    y = xf * lax.rsqrt(jnp.mean(xf * xf, axis=-1, keepdims=True) + NORM_EPS)
    return (y * w.astype(jnp.float32)).astype(x.dtype)


def _rope_axis(x, pos):
    n = x.shape[-1]
    half = n // 2
    freqs = ROPE_THETA ** (-jnp.arange(half, dtype=jnp.float32) * (2.0 / n))
    ang = pos.astype(jnp.float32)[:, None] * freqs[None, :]
    cos = jnp.cos(ang)[None, :, None, :]
    sin = jnp.sin(ang)[None, :, None, :]
    xf = x.astype(jnp.float32)
    x1, x2 = xf[..., :half], xf[..., half:]
    return jnp.concatenate([x1 * cos - x2 * sin, x2 * cos + x1 * sin], axis=-1).astype(x.dtype)


def rope_2d(x, row, col):
    h = x.shape[-1] // 2
    return jnp.concatenate([_rope_axis(x[..., :h], row), _rope_axis(x[..., h:], col)], axis=-1)


def pool_mix(u, w_pool, s_pool):
    B, L, _ = u.shape
    uf = u.astype(jnp.float32)
    cs = jnp.concatenate([jnp.zeros((B, 1, POOL_WIDTH), jnp.float32), jnp.cumsum(uf, axis=1)], axis=1)
    t = jnp.arange(L)
    outs = []
    for gi, w in enumerate(POOL_WINDOWS):
        lo = jnp.clip(t - w // 2, 0, L)
        hi = jnp.clip(t + w - w // 2, 0, L)
        csg = cs[..., gi * POOL_GROUP:(gi + 1) * POOL_GROUP]
        cnt = (hi - lo).astype(jnp.float32)[None, :, None]
        mean = (csg[:, hi] - csg[:, lo]) / cnt
        outs.append(mean - uf[..., gi * POOL_GROUP:(gi + 1) * POOL_GROUP])
    y = jnp.stack(outs, axis=2).astype(u.dtype)
    y = jnp.einsum('blgc,gcd->blgd', y, w_pool).reshape(B, L, POOL_WIDTH)
    return y * s_pool


def _attend(q, k, v):
    s = jnp.einsum('bqhgd,bshd->bhgqs', q, k).astype(jnp.float32) * (HEAD_DIM ** -0.5)
    p = jax.nn.softmax(s, axis=-1).astype(v.dtype)
    return jnp.einsum('bhgqs,bshd->bqhgd', p, v)


def latent_attention(q, k, v, k_ctx, v_ctx):
    B, L = q.shape[:2]
    k_all = jnp.concatenate([k, k_ctx], axis=1)
    v_all = jnp.concatenate([v, v_ctx], axis=1)
    nb = L // Q_BLOCK
    qb = jnp.moveaxis(q.reshape(B, nb, Q_BLOCK, N_KV_HEADS, Q_PER_KV, HEAD_DIM), 1, 0)
    ob = lax.map(lambda blk: _attend(blk, k_all, v_all), qb)
    return jnp.moveaxis(ob, 0, 1).reshape(B, L, ATTN_WIDTH)


def _hgrn_gates(z, lb):
    z = z.astype(jnp.float32)
    log_f = jnp.logaddexp(jnp.log(lb), jnp.log1p(-lb) + jax.nn.log_sigmoid(z))
    one_minus_f = (1.0 - lb) * jax.nn.sigmoid(-z)
    return log_f, one_minus_f


def hgrn_scan(q, k, v, log_f, s0):
    B, L, H, K = q.shape
    V = v.shape[-1]
    C = HG_CHUNK
    N = L // C
    q = q.reshape(B, N, C, H, K)
    k = k.reshape(B, N, C, H, K)
    v = v.reshape(B, N, C, H, V)
    G = jnp.cumsum(log_f.reshape(B, N, C, H, K), axis=2)
    G_last = G[:, :, -1]
    mask = jnp.tril(jnp.ones((C, C), dtype=bool))[None, None, :, :, None, None]
    diff = G[:, :, :, None] - G[:, :, None, :]
    decay = jnp.exp(jnp.where(mask, diff, -jnp.inf))
    A = jnp.einsum('bnthk,bnshk,bntshk->bnhts', q, k, decay)
    o_intra = jnp.einsum('bnhts,bnshv->bnthv', A, v)
    k_dec = k * jnp.exp(G_last[:, :, None] - G)
    kv = jnp.einsum('bnchk,bnchv->bnhkv', k_dec, v)
    a_chunk = jnp.exp(G_last)

    def step(S, inp):
        a, u = inp
        return a[..., None] * S + u, S

    s_fin, s_start = lax.scan(step, s0, (jnp.moveaxis(a_chunk, 1, 0), jnp.moveaxis(kv, 1, 0)))
    s_start = jnp.moveaxis(s_start, 0, 1)
    o_inter = jnp.einsum('bnchk,bnhkv->bnchv', q * jnp.exp(G), s_start)
    return (o_intra + o_inter).reshape(B, L, H, V), s_fin


def hgrn_bidir(P, s0_f, s0_b):
    o_f, s_f = hgrn_scan(P['hq'], P['k_f'], P['hv'], P['logf_f'], s0_f)
    rev = lambda a: a[:, ::-1]
    o_b, s_b = hgrn_scan(rev(P['hq']), rev(P['k_b']), rev(P['hv']), rev(P['logf_b']), s0_b)
    return o_f + rev(o_b), s_f, s_b


def hgrn_readout(o, gate, norm_w):
    B, L = o.shape[:2]
    return rmsnorm(o, norm_w).reshape(B, L, HG_WIDTH) * jax.nn.silu(gate)


def in_projection(h, w_in_l, q_norm_l, k_norm_l, lb_l, row=None, col=None):
    B, L, _ = h.shape
    (u_pool, q, k, v, hq, hi, zf_f, zf_b, hgate, g_pool, g_attn, g_hg) = jnp.split(h @ w_in_l, IN_SPLITS, axis=-1)
    q = rmsnorm(q.reshape(B, L, N_HEADS, HEAD_DIM), q_norm_l)
    k = rmsnorm(k.reshape(B, L, N_KV_HEADS, HEAD_DIM), k_norm_l)
    if row is not None:
        q = rope_2d(q, row, col)
        k = rope_2d(k, row, col)
    hs = lambda a: a.reshape(B, L, HG_HEADS, HG_DK)
    logf_f, k_f = _hgrn_gates(hs(zf_f), lb_l[0].reshape(HG_HEADS, HG_DK))
    logf_b, k_b = _hgrn_gates(hs(zf_b), lb_l[1].reshape(HG_HEADS, HG_DK))
    return {
        'pool': u_pool,
        'q': q.reshape(B, L, N_KV_HEADS, Q_PER_KV, HEAD_DIM),
        'k': k,
        'v': v.reshape(B, L, N_KV_HEADS, HEAD_DIM),
        'hq': hs(hq).astype(jnp.float32),
        'hv': hi.reshape(B, L, HG_HEADS, HG_DV),
        'logf_f': logf_f, 'k_f': k_f, 'logf_b': logf_b, 'k_b': k_b,
        'hgate': hgate, 'g_pool': g_pool, 'g_attn': g_attn, 'g_hg': g_hg,
    }


def merge_branches(P, pool_out, attn_out, hg_out, wbp, wba, wbh, wo):
    y = (jax.nn.sigmoid(P['g_pool']) * (pool_out @ wbp)
         + jax.nn.sigmoid(P['g_attn']) * (attn_out @ wba)
         + jax.nn.sigmoid(P['g_hg']) * (hg_out @ wbh))
    return y @ wo


def swiglu(h, w1, w2):
    a, b = jnp.split(h @ w1, 2, axis=-1)
    return (jax.nn.silu(a) * b) @ w2


def setup_inputs(seed: int = 0) -> dict:
    key = jax.random.key(seed)
    ks = jax.random.split(key, 24)
    f32 = jnp.float32
    nrm = lambda k, shape, s: jax.random.normal(k, shape, f32) * s
    return {
        'x': nrm(ks[0], (BATCH, SEQ, D_MODEL), 1.0),
        'c': nrm(ks[1], (BATCH, D_MODEL), 1.0),
        'ctx': nrm(ks[2], (BATCH, CTX_LEN, D_MODEL), 1.0),
        'c_ctx': nrm(ks[3], (D_MODEL,), 1.0),
        'w_ada': nrm(ks[4], (DEPTH, D_MODEL, N_MOD * D_MODEL), 0.5 * D_MODEL ** -0.5),
        'b_ada': nrm(ks[5], (DEPTH, N_MOD * D_MODEL), 0.01),
        'norm1_w': 1.0 + nrm(ks[6], (DEPTH, D_MODEL), 0.02),
        'w_in': nrm(ks[7], (DEPTH, D_MODEL, IN_WIDTH), D_MODEL ** -0.5),
        'pool_w': nrm(ks[8], (DEPTH, len(POOL_WINDOWS), POOL_GROUP, POOL_GROUP), POOL_GROUP ** -0.5),
        'pool_scale': 1.0 + nrm(ks[9], (DEPTH, POOL_WIDTH), 0.1),
        'q_norm_w': 1.0 + nrm(ks[10], (DEPTH, HEAD_DIM), 0.02),
        'k_norm_w': 1.0 + nrm(ks[11], (DEPTH, HEAD_DIM), 0.02),
        'hg_lb_logits': nrm(ks[12], (DEPTH, 2, HG_WIDTH), 0.5),
        'hg_norm_w': 1.0 + nrm(ks[13], (DEPTH, HG_DV), 0.02),
        'w_branch_pool': nrm(ks[14], (DEPTH, POOL_WIDTH, D_MODEL), POOL_WIDTH ** -0.5),
        'w_branch_attn': nrm(ks[15], (DEPTH, ATTN_WIDTH, D_MODEL), ATTN_WIDTH ** -0.5),
        'w_branch_hg': nrm(ks[16], (DEPTH, HG_WIDTH, D_MODEL), HG_WIDTH ** -0.5),
        'w_out': nrm(ks[17], (DEPTH, D_MODEL, D_MODEL), D_MODEL ** -0.5),
        'norm2_w': 1.0 + nrm(ks[18], (DEPTH, D_MODEL), 0.02),
        'w_ffn_in': nrm(ks[19], (DEPTH, D_MODEL, 2 * D_FF), D_MODEL ** -0.5),
        'w_ffn_out': nrm(ks[20], (DEPTH, D_FF, D_MODEL), D_FF ** -0.5),
    }


def reference(x, c, ctx, c_ctx, w_ada, b_ada, norm1_w, w_in, pool_w, pool_scale,
              q_norm_w, k_norm_w, hg_lb_logits, hg_norm_w, w_branch_pool, w_branch_attn,
              w_branch_hg, w_out, norm2_w, w_ffn_in, w_ffn_out):
    B, L, _ = x.shape
    ROWS = L // GRID_W
    row = jnp.repeat(jnp.arange(ROWS, dtype=jnp.int32), GRID_W)
    col = jnp.tile(jnp.arange(GRID_W, dtype=jnp.int32), ROWS)
    lb_cum = jnp.cumsum(jax.nn.softmax(hg_lb_logits.astype(jnp.float32), axis=0), axis=0)
    lb_all = lb_cum - lb_cum[0]
    c_act = jax.nn.silu(c)
    cc_act = jax.nn.silu(c_ctx)
    xc = ctx
    s_zero = jnp.zeros((B, HG_HEADS, HG_DK, HG_DV), jnp.float32)
    for l in range(DEPTH):
        sh1, sc1, g1, sh2, sc2, g2 = jnp.split((c_act @ w_ada[l] + b_ada[l])[:, None, :], N_MOD, axis=-1)
        sh1c, sc1c, g1c, sh2c, sc2c, g2c = jnp.split(cc_act @ w_ada[l] + b_ada[l], N_MOD, axis=-1)

        h = rmsnorm(x, norm1_w[l]) * (1.0 + sc1) + sh1
        hc = rmsnorm(xc, norm1_w[l]) * (1.0 + sc1c) + sh1c
        P = in_projection(h, w_in[l], q_norm_w[l], k_norm_w[l], lb_all[l], row, col)
        Pc = in_projection(hc, w_in[l], q_norm_w[l], k_norm_w[l], lb_all[l])

        o_hc, s_f, s_b = hgrn_bidir(Pc, s_zero, s_zero)
        o_hl, _, _ = hgrn_bidir(P, s_f, s_b)

        y_lat = merge_branches(
            P,
            pool_mix(P['pool'], pool_w[l], pool_scale[l]),
            latent_attention(P['q'], P['k'], P['v'], Pc['k'], Pc['v']),
            hgrn_readout(o_hl, P['hgate'], hg_norm_w[l]),
            w_branch_pool[l], w_branch_attn[l], w_branch_hg[l], w_out[l])
        x_new = x + g1 * y_lat

        if l < DEPTH - 1:
            Lc = xc.shape[1]
            y_ctx = merge_branches(
                Pc,
                pool_mix(Pc['pool'], pool_w[l], pool_scale[l]),
                _attend(Pc['q'], Pc['k'], Pc['v']).reshape(B, Lc, ATTN_WIDTH),
                hgrn_readout(o_hc, Pc['hgate'], hg_norm_w[l]),
                w_branch_pool[l], w_branch_attn[l], w_branch_hg[l], w_out[l])
            xc = xc + g1c * y_ctx
        x = x_new

        h2 = rmsnorm(x, norm2_w[l]) * (1.0 + sc2) + sh2
        x = x + g2 * swiglu(h2, w_ffn_in[l], w_ffn_out[l])
        if l < DEPTH - 1:
            h2c = rmsnorm(xc, norm2_w[l]) * (1.0 + sc2c) + sh2c
            xc = xc + g2c * swiglu(h2c, w_ffn_in[l], w_ffn_out[l])
    return x
```

```python
import functools

import jax
import jax.numpy as jnp
import numpy as np
from jax import lax
from jax.experimental import pallas as pl
from jax.experimental.pallas import tpu as pltpu

D_MODEL = 1024
BATCH = 8
SEQ = 2048
DEPTH = 4
GRID_W = 64
CTX_LEN = 256
T_ALL = CTX_LEN + SEQ
NORM_EPS = 1e-6
N_MOD = 6

POOL_WINDOWS = (2, 4, 8, 16)
POOL_GROUP = 64
POOL_WIDTH = 256
N_HEADS = 8
N_KV_HEADS = 2
HEAD_DIM = 64
Q_PER_KV = 4
ATTN_WIDTH = 512
KV_WIDTH = 128
ROPE_THETA = 10000.0
HG_HEADS = 4
HG_DK = 64
HG_DV = 64
HG_WIDTH = 256
HG_CHUNK = 16
D_FF = 2816
IN_WIDTH = 5376

C_POOL = 0
C_QKV = 256
C_HG = 1024
C_GATE = 2304

MOD_ROWS = 16
TM = 256
VMEM_LIMIT = 56 * 1024 * 1024

BF16 = jnp.bfloat16
F32 = jnp.float32


def _dot(a, b):
    return jnp.dot(a, b, preferred_element_type=F32)


def _sigmoid(x):
    return 1.0 / (1.0 + jnp.exp(-x))


def _ada_kernel(c_ref, w_ref, b_ref, o_ref):
    c = c_ref[...]
    act = (c * _sigmoid(c)).astype(BF16)
    o_ref[...] = _dot(act, w_ref[...].astype(BF16)) + b_ref[...]


def _ada_call(cc, w_ada, b_ada):
    tn = 1536
    n = N_MOD * D_MODEL
    return pl.pallas_call(
        _ada_kernel,
        out_shape=jax.ShapeDtypeStruct((DEPTH, MOD_ROWS, n), F32),
        grid=(DEPTH, n // tn),
        in_specs=[
            pl.BlockSpec((MOD_ROWS, D_MODEL), lambda l, j: (0, 0)),
            pl.BlockSpec((None, D_MODEL, tn), lambda l, j: (l, 0, j)),
            pl.BlockSpec((None, 1, tn), lambda l, j: (l, 0, j)),
        ],
        out_specs=pl.BlockSpec((None, MOD_ROWS, tn), lambda l, j: (l, 0, j)),
        compiler_params=pltpu.CompilerParams(
            dimension_semantics=("arbitrary", "arbitrary"), vmem_limit_bytes=VMEM_LIMIT),
        name="ada_mod",
    )(cc, w_ada, b_ada.reshape(DEPTH, 1, n))


def _mod_spec(l):
    return pl.BlockSpec((None, None, N_MOD, D_MODEL),
                        lambda b, t: (l, jnp.where(t == 0, BATCH, b), 0, 0))


def _const_spec(shape):
    nd = len(shape)
    return pl.BlockSpec(shape, lambda b, t: (0,) * nd, pipeline_mode=pl.Buffered(1))


def _tok_spec(width):
    return pl.BlockSpec((None, TM, width), lambda b, t: (b, t, 0))


def _rms_mod(x, nw, scale, shift):
    ms = jnp.mean(x * x, axis=-1, keepdims=True)
    y = x * lax.rsqrt(ms + NORM_EPS) * nw
    return y * (1.0 + scale) + shift


def _inproj_kernel(x_ref, mod_ref, nw_ref, w_ref, pool_ref, qkv_ref, hg_ref, gate_ref):
    h = _rms_mod(x_ref[...], nw_ref[...], mod_ref[1:2, :], mod_ref[0:1, :]).astype(BF16)
    pool_ref[...] = _dot(h, w_ref[:, C_POOL:C_QKV])
    qkv_ref[...] = _dot(h, w_ref[:, C_QKV:C_HG])
    hg_ref[...] = _dot(h, w_ref[:, C_HG:C_GATE])
    gate_ref[...] = _sigmoid(_dot(h, w_ref[:, C_GATE:IN_WIDTH])).astype(BF16)


def _inproj_call(xs, mods, nw, w_in, l):
    widths = (C_QKV - C_POOL, C_HG - C_QKV, C_GATE - C_HG, IN_WIDTH - C_GATE)
    dtypes = (F32, F32, F32, BF16)
    return pl.pallas_call(
        _inproj_kernel,
        out_shape=[jax.ShapeDtypeStruct((BATCH, T_ALL, w), dt) for w, dt in zip(widths, dtypes)],
        grid=(BATCH, T_ALL // TM),
        in_specs=[_tok_spec(D_MODEL), _mod_spec(l), _const_spec((1, D_MODEL)),
                  _const_spec((D_MODEL, IN_WIDTH))],
        out_specs=[_tok_spec(w) for w in widths],
        compiler_params=pltpu.CompilerParams(
            dimension_semantics=("arbitrary", "arbitrary"), vmem_limit_bytes=VMEM_LIMIT),
        name="in_proj",
    )(xs, mods, nw.reshape(1, D_MODEL), w_in)


def _merge_kernel(pool_ref, attn_ref, hgo_ref, gate_ref, x_ref, mod_ref,
                  wbp_ref, wba_ref, wbh_ref, wo_ref, o_ref):
    y = (gate_ref[:, 0:D_MODEL].astype(F32) * _dot(pool_ref[...], wbp_ref[...])
         + gate_ref[:, D_MODEL:2 * D_MODEL].astype(F32) * _dot(attn_ref[...], wba_ref[...])
         + gate_ref[:, 2 * D_MODEL:3 * D_MODEL].astype(F32) * _dot(hgo_ref[...], wbh_ref[...]))
    z = _dot(y.astype(BF16), wo_ref[...])
    o_ref[...] = x_ref[...] + mod_ref[2:3, :] * z


def _merge_call(pool_o, attn_o, hg_o, gates, xs, mods, wbp, wba, wbh, wo, l):
    return pl.pallas_call(
        _merge_kernel,
        out_shape=jax.ShapeDtypeStruct((BATCH, T_ALL, D_MODEL), F32),
        grid=(BATCH, T_ALL // TM),
        in_specs=[_tok_spec(POOL_WIDTH), _tok_spec(ATTN_WIDTH), _tok_spec(HG_WIDTH),
                  _tok_spec(3 * D_MODEL), _tok_spec(D_MODEL), _mod_spec(l),
                  _const_spec((POOL_WIDTH, D_MODEL)), _const_spec((ATTN_WIDTH, D_MODEL)),
                  _const_spec((HG_WIDTH, D_MODEL)), _const_spec((D_MODEL, D_MODEL))],
        out_specs=_tok_spec(D_MODEL),
        compiler_params=pltpu.CompilerParams(
            dimension_semantics=("arbitrary", "arbitrary"), vmem_limit_bytes=VMEM_LIMIT),
        name="merge",
    )(pool_o, attn_o, hg_o, gates, xs, mods, wbp, wba, wbh, wo)


def _ffn_kernel(x_ref, mod_ref, nw_ref, w1_ref, w2_ref, o_ref):
    x = x_ref[...]
    h = _rms_mod(x, nw_ref[...], mod_ref[4:5, :], mod_ref[3:4, :]).astype(BF16)
    a = _dot(h, w1_ref[:, 0:D_FF])
    b = _dot(h, w1_ref[:, D_FF:2 * D_FF])
    s = (a * _sigmoid(a) * b).astype(BF16)
    o_ref[...] = x + mod_ref[5:6, :] * _dot(s, w2_ref[...])


def _ffn_call(xs, mods, nw, w1, w2, l):
    return pl.pallas_call(
        _ffn_kernel,
        out_shape=jax.ShapeDtypeStruct((BATCH, T_ALL, D_MODEL), F32),
        grid=(BATCH, T_ALL // TM),
        in_specs=[_tok_spec(D_MODEL), _mod_spec(l), _const_spec((1, D_MODEL)),
                  _const_spec((D_MODEL, 2 * D_FF)), _const_spec((D_FF, D_MODEL))],
        out_specs=_tok_spec(D_MODEL),
        compiler_params=pltpu.CompilerParams(
            dimension_semantics=("arbitrary", "arbitrary"), vmem_limit_bytes=VMEM_LIMIT),
        name="ffn",
    )(xs, mods, nw.reshape(1, D_MODEL), w1, w2)


def _rmsnorm(x, w):
    xf = x.astype(F32)
    return xf * lax.rsqrt(jnp.mean(xf * xf, axis=-1, keepdims=True) + NORM_EPS) * w


def _rope_axis(x, pos):
    n = x.shape[-1]
    half = n // 2
    freqs = ROPE_THETA ** (-jnp.arange(half, dtype=F32) * (2.0 / n))
    ang = pos.astype(F32)[:, None] * freqs[None, :]
    cos = jnp.cos(ang)[None, :, None, :]
    sin = jnp.sin(ang)[None, :, None, :]
    x1, x2 = x[..., :half], x[..., half:]
    return jnp.concatenate([x1 * cos - x2 * sin, x2 * cos + x1 * sin], axis=-1)


def _rope_2d(x, row, col):
    h = x.shape[-1] // 2
    return jnp.concatenate([_rope_axis(x[..., :h], row), _rope_axis(x[..., h:], col)], axis=-1)


def _pool_mix(u, w_pool, s_pool):
    Bn, Ln, _ = u.shape
    cs = jnp.concatenate([jnp.zeros((Bn, 1, POOL_WIDTH), F32), jnp.cumsum(u, axis=1)], axis=1)
    t = jnp.arange(Ln)
    outs = []
    for gi, w in enumerate(POOL_WINDOWS):
        lo = jnp.clip(t - w // 2, 0, Ln)
        hi = jnp.clip(t + w - w // 2, 0, Ln)
        csg = cs[..., gi * POOL_GROUP:(gi + 1) * POOL_GROUP]
        cnt = (hi - lo).astype(F32)[None, :, None]
        mean = (csg[:, hi] - csg[:, lo]) / cnt
        outs.append(mean - u[..., gi * POOL_GROUP:(gi + 1) * POOL_GROUP])
    y = jnp.stack(outs, axis=2)
    y = jnp.einsum('blgc,gcd->blgd', y, w_pool).reshape(Bn, Ln, POOL_WIDTH)
    return y * s_pool


def _attend(q, k, v):
    s = jnp.einsum('bqhgd,bshd->bhgqs', q, k).astype(F32) * (HEAD_DIM ** -0.5)
    p = jax.nn.softmax(s, axis=-1)
    return jnp.einsum('bhgqs,bshd->bqhgd', p, v)


def _hgrn_gates(z, lb):
    log_f = jnp.logaddexp(jnp.log(lb), jnp.log1p(-lb) + jax.nn.log_sigmoid(z))
    return log_f, (1.0 - lb) * jax.nn.sigmoid(-z)


def _hgrn_scan(q, k, v, log_f, s0):
    Bn, Ln, H, K = q.shape
    V = v.shape[-1]
    C = HG_CHUNK
    N = Ln // C
    q = q.reshape(Bn, N, C, H, K)
    k = k.reshape(Bn, N, C, H, K)
    v = v.reshape(Bn, N, C, H, V)
    G = jnp.cumsum(log_f.reshape(Bn, N, C, H, K), axis=2)
    G_last = G[:, :, -1]
    mask = jnp.tril(jnp.ones((C, C), dtype=bool))[None, None, :, :, None, None]
    diff = G[:, :, :, None] - G[:, :, None, :]
    decay = jnp.exp(jnp.where(mask, diff, -jnp.inf))
    A = jnp.einsum('bnthk,bnshk,bntshk->bnhts', q, k, decay)
    o_intra = jnp.einsum('bnhts,bnshv->bnthv', A, v)
    k_dec = k * jnp.exp(G_last[:, :, None] - G)
    kv = jnp.einsum('bnchk,bnchv->bnhkv', k_dec, v)
    a_chunk = jnp.exp(G_last)

    def step(S, inp):
        a, u = inp
        return a[..., None] * S + u, S

    s_fin, s_start = lax.scan(step, s0, (jnp.moveaxis(a_chunk, 1, 0), jnp.moveaxis(kv, 1, 0)))
    s_start = jnp.moveaxis(s_start, 0, 1)
    o_inter = jnp.einsum('bnchk,bnhkv->bnchv', q * jnp.exp(G), s_start)
    return (o_intra + o_inter).reshape(Bn, Ln, H, V), s_fin


def _hgrn_bidir(hq, hv, zf, zb, lb, s0_f, s0_b):
    hs = lambda a: a.reshape(a.shape[0], a.shape[1], HG_HEADS, HG_DK)
    logf_f, k_f = _hgrn_gates(hs(zf), lb[0].reshape(HG_HEADS, HG_DK))
    logf_b, k_b = _hgrn_gates(hs(zb), lb[1].reshape(HG_HEADS, HG_DK))
    rev = lambda a: a[:, ::-1]
    o_f, s_f = _hgrn_scan(hs(hq), k_f, hs(hv), logf_f, s0_f)
    o_b, s_b = _hgrn_scan(rev(hs(hq)), rev(k_b), rev(hs(hv)), rev(logf_b), s0_b)
    return o_f + rev(o_b), s_f, s_b


def _mixers_jnp(pool_u, qkv, hg5, pool_w, pool_scale, qn, kn, lb, hg_norm_w, row, col):
    Bn = pool_u.shape[0]
    pool_o = jnp.concatenate([_pool_mix(pool_u[:, :CTX_LEN], pool_w, pool_scale),
                              _pool_mix(pool_u[:, CTX_LEN:], pool_w, pool_scale)], axis=1)
    q = _rmsnorm(qkv[..., :ATTN_WIDTH].reshape(Bn, T_ALL, N_HEADS, HEAD_DIM), qn)
    k = _rmsnorm(qkv[..., ATTN_WIDTH:ATTN_WIDTH + KV_WIDTH].reshape(Bn, T_ALL, N_KV_HEADS, HEAD_DIM), kn)
    v = qkv[..., ATTN_WIDTH + KV_WIDTH:].reshape(Bn, T_ALL, N_KV_HEADS, HEAD_DIM)
    q = jnp.concatenate([q[:, :CTX_LEN], _rope_2d(q[:, CTX_LEN:], row, col)], axis=1)
    k = jnp.concatenate([k[:, :CTX_LEN], _rope_2d(k[:, CTX_LEN:], row, col)], axis=1)
    q = q.reshape(Bn, T_ALL, N_KV_HEADS, Q_PER_KV, HEAD_DIM)
    a_lat = _attend(q[:, CTX_LEN:], k, v).reshape(Bn, SEQ, ATTN_WIDTH)
    a_ctx = _attend(q[:, :CTX_LEN], k[:, :CTX_LEN], v[:, :CTX_LEN]).reshape(Bn, CTX_LEN, ATTN_WIDTH)
    attn_o = jnp.concatenate([a_ctx, a_lat], axis=1)

    hq, hi, zf, zb, hgate = [hg5[..., i * HG_WIDTH:(i + 1) * HG_WIDTH] for i in range(5)]
    s_zero = jnp.zeros((Bn, HG_HEADS, HG_DK, HG_DV), F32)
    c = slice(0, CTX_LEN)
    o_c, s_f, s_b = _hgrn_bidir(hq[:, c], hi[:, c], zf[:, c], zb[:, c], lb, s_zero, s_zero)
    t = slice(CTX_LEN, T_ALL)
    o_l, _, _ = _hgrn_bidir(hq[:, t], hi[:, t], zf[:, t], zb[:, t], lb, s_f, s_b)
    o = jnp.concatenate([o_c, o_l], axis=1)
    hg_o = _rmsnorm(o, hg_norm_w).reshape(Bn, T_ALL, HG_WIDTH) * jax.nn.silu(hgate)
    return pool_o.astype(BF16), attn_o.astype(BF16), hg_o.astype(BF16)


def kernel(x, c, ctx, c_ctx, w_ada, b_ada, norm1_w, w_in, pool_w, pool_scale, q_norm_w, k_norm_w,
           hg_lb_logits, hg_norm_w, w_branch_pool, w_branch_attn, w_branch_hg, w_out, norm2_w,
           w_ffn_in, w_ffn_out):
    rows = SEQ // GRID_W
    row = jnp.repeat(jnp.arange(rows, dtype=jnp.int32), GRID_W)
    col = jnp.tile(jnp.arange(GRID_W, dtype=jnp.int32), rows)
    lb_cum = jnp.cumsum(jax.nn.softmax(hg_lb_logits.astype(F32), axis=0), axis=0)
    lb_all = lb_cum - lb_cum[0]

    cc = jnp.zeros((MOD_ROWS, D_MODEL), F32).at[:BATCH].set(c).at[BATCH].set(c_ctx)
    mods = _ada_call(cc, w_ada, b_ada).reshape(DEPTH, MOD_ROWS, N_MOD, D_MODEL)

    xs = jnp.concatenate([ctx, x], axis=1)
    for l in range(DEPTH):
        pool_u, qkv, hg5, gates = _inproj_call(xs, mods, norm1_w[l], w_in[l].astype(BF16), l)
        pool_o, attn_o, hg_o = _mixers_jnp(pool_u, qkv, hg5, pool_w[l], pool_scale[l], q_norm_w[l],
                                           k_norm_w[l], lb_all[l], hg_norm_w[l], row, col)
        xs = _merge_call(pool_o, attn_o, hg_o, gates, xs, mods,
                         w_branch_pool[l].astype(BF16), w_branch_attn[l].astype(BF16),
                         w_branch_hg[l].astype(BF16), w_out[l].astype(BF16), l)
        xs = _ffn_call(xs, mods, norm2_w[l], w_ffn_in[l].astype(BF16), w_ffn_out[l].astype(BF16), l)
    return xs[:, CTX_LEN:]
```

```python
import jax
import jax.numpy as jnp
from jax import lax
from jax.experimental import pallas as pl
from jax.experimental.pallas import tpu as pltpu

D_MODEL = 1024
BATCH = 8
SEQ = 2048
DEPTH = 4
GRID_W = 64
CTX_LEN = 256
T_ALL = CTX_LEN + SEQ
NORM_EPS = 1e-6
N_MOD = 6

POOL_GROUP = 64
POOL_WIDTH = 256
N_HEADS = 8
N_KV_HEADS = 2
HEAD_DIM = 64
Q_PER_KV = N_HEADS // N_KV_HEADS
ATTN_WIDTH = N_HEADS * HEAD_DIM
KV_WIDTH = N_KV_HEADS * HEAD_DIM
ROPE_THETA = 10000.0
ROPE_HALF = HEAD_DIM // 4
HG_HEADS = 4
HG_DK = 64
HG_WIDTH = 256
HG_CHUNK = 16
N_CHUNKS = T_ALL // HG_CHUNK
CTX_CHUNKS = CTX_LEN // HG_CHUNK
D_FF = 2816
IN_WIDTH = 5376

C_POOL = 0
C_QKV = 256
C_HG = 1024
C_GATE = 2304

MOD_ROWS = 16
TM = 256
VMEM_LIMIT = 56 * 1024 * 1024

BF16 = jnp.bfloat16
F32 = jnp.float32


def _dot(a, b):
    return jnp.dot(a, b, preferred_element_type=F32)


def _dot_nt(a, b):
    return lax.dot_general(a, b, (((1,), (1,)), ((), ())), preferred_element_type=F32)


def _sigmoid(x):
    return 1.0 / (1.0 + jnp.exp(-x))


def _params(n_grid):
    return pltpu.CompilerParams(dimension_semantics=("arbitrary",) * n_grid,
                                vmem_limit_bytes=VMEM_LIMIT)


def _ada_kernel(c_ref, w_ref, b_ref, o_ref):
    c = c_ref[...]
    act = (c * _sigmoid(c)).astype(BF16)
    o_ref[...] = _dot(act, w_ref[...].astype(BF16)) + b_ref[...]


def _ada_call(cc, w_ada, b_ada):
    tn = 1536
    n = N_MOD * D_MODEL
    return pl.pallas_call(
        _ada_kernel,
        out_shape=jax.ShapeDtypeStruct((DEPTH, MOD_ROWS, n), F32),
        grid=(DEPTH, n // tn),
        in_specs=[
            pl.BlockSpec((MOD_ROWS, D_MODEL), lambda l, j: (0, 0)),
            pl.BlockSpec((None, D_MODEL, tn), lambda l, j: (l, 0, j)),
            pl.BlockSpec((None, 1, tn), lambda l, j: (l, 0, j)),
        ],
        out_specs=pl.BlockSpec((None, MOD_ROWS, tn), lambda l, j: (l, 0, j)),
        compiler_params=_params(2),
        name="ada_mod",
    )(cc, w_ada, b_ada.reshape(DEPTH, 1, n))


def _mod_spec(l, nb):
    return pl.BlockSpec((None, None, N_MOD, D_MODEL),
                        lambda b, t: (l, jnp.where(t == 0, BATCH, b), 0, 0))


def _const_spec(shape):
    nd = len(shape)
    return pl.BlockSpec(shape, lambda *_: (0,) * nd, pipeline_mode=pl.Buffered(1))


def _tok_spec(width):
    return pl.BlockSpec((None, TM, width), lambda b, t: (b, t, 0))


def _rms_mod(x, nw, scale, shift):
    ms = jnp.mean(x * x, axis=-1, keepdims=True)
    y = x * lax.rsqrt(ms + NORM_EPS) * nw
    return y * (1.0 + scale) + shift


def _head_mean_sq(x, ones_bd):
    x2 = x * x
    hi = x2.astype(BF16)
    lo = (x2 - hi.astype(F32)).astype(BF16)
    return (_dot(hi, ones_bd) + _dot(lo, ones_bd)) * (1.0 / HEAD_DIM)


def _swap_rope_halves(x):
    w = x.shape[1]
    lane = lax.broadcasted_iota(jnp.int32, x.shape, 1)
    up = pltpu.roll(x, w - ROPE_HALF, 1)
    dn = pltpu.roll(x, ROPE_HALF, 1)
    return jnp.where((lane & ROPE_HALF) == 0, up, dn)


def _inproj_kernel(x_ref, mod_ref, nw_ref, w_ref, qw_ref, kw_ref, cos_ref, sin_ref, ones_ref,
                   pool_ref, q_ref, k_ref, v_ref, hg_ref, gate_ref):
    h = _rms_mod(x_ref[...], nw_ref[...], mod_ref[1:2, :], mod_ref[0:1, :]).astype(BF16)
    pool_ref[...] = _dot(h, w_ref[:, C_POOL:C_QKV])

    qkv = _dot(h, w_ref[:, C_QKV:C_HG])
    cos = cos_ref[...]
    sin = sin_ref[...]
    q = qkv[:, 0:ATTN_WIDTH]
    q = q * lax.rsqrt(_head_mean_sq(q, ones_ref[...]) + NORM_EPS) * qw_ref[...]
    cos_q = jnp.concatenate([cos] * (ATTN_WIDTH // KV_WIDTH), axis=1)
    sin_q = jnp.concatenate([sin] * (ATTN_WIDTH // KV_WIDTH), axis=1)
    q = (q * cos_q + _swap_rope_halves(q) * sin_q) * (HEAD_DIM ** -0.5)
    for hh in range(N_HEADS):
        q_ref[hh] = q[:, hh * HEAD_DIM:(hh + 1) * HEAD_DIM].astype(BF16)
    k = qkv[:, ATTN_WIDTH:ATTN_WIDTH + KV_WIDTH]
    k = k * lax.rsqrt(_head_mean_sq(k, ones_ref[0:KV_WIDTH, 0:KV_WIDTH]) + NORM_EPS) * kw_ref[...]
    k = k * cos + _swap_rope_halves(k) * sin
    v = qkv[:, ATTN_WIDTH + KV_WIDTH:]
    for hh in range(N_KV_HEADS):
        k_ref[hh] = k[:, hh * HEAD_DIM:(hh + 1) * HEAD_DIM].astype(BF16)
        v_ref[hh] = v[:, hh * HEAD_DIM:(hh + 1) * HEAD_DIM].astype(BF16)

    hg_ref[...] = _dot(h, w_ref[:, C_HG:C_GATE])
    gate_ref[...] = _sigmoid(_dot(h, w_ref[:, C_GATE:IN_WIDTH])).astype(BF16)


def _inproj_call(xs, mods, nw, w_in, qw, kw, cos_t, sin_t, ones_bd, l):
    nb = xs.shape[0]
    head_spec = lambda nh: pl.BlockSpec((None, nh, TM, HEAD_DIM), lambda b, t: (b, 0, t, 0))
    rope_spec = pl.BlockSpec((TM, KV_WIDTH), lambda b, t: (t, 0))
    out_shape = [
        jax.ShapeDtypeStruct((nb, T_ALL, POOL_WIDTH), F32),
        jax.ShapeDtypeStruct((nb, N_HEADS, T_ALL, HEAD_DIM), BF16),
        jax.ShapeDtypeStruct((nb, N_KV_HEADS, T_ALL, HEAD_DIM), BF16),
        jax.ShapeDtypeStruct((nb, N_KV_HEADS, T_ALL, HEAD_DIM), BF16),
        jax.ShapeDtypeStruct((nb, T_ALL, C_GATE - C_HG), F32),
        jax.ShapeDtypeStruct((nb, T_ALL, IN_WIDTH - C_GATE), BF16),
    ]
    out_specs = [_tok_spec(POOL_WIDTH), head_spec(N_HEADS), head_spec(N_KV_HEADS),
                 head_spec(N_KV_HEADS), _tok_spec(C_GATE - C_HG), _tok_spec(IN_WIDTH - C_GATE)]
    return pl.pallas_call(
        _inproj_kernel,
        out_shape=out_shape,
        grid=(nb, T_ALL // TM),
        in_specs=[_tok_spec(D_MODEL), _mod_spec(l, nb), _const_spec((1, D_MODEL)),
                  _const_spec((D_MODEL, IN_WIDTH)), _const_spec((1, ATTN_WIDTH)),
                  _const_spec((1, KV_WIDTH)), rope_spec, rope_spec,
                  _const_spec((ATTN_WIDTH, ATTN_WIDTH))],
        out_specs=out_specs,
        compiler_params=_params(2),
        name="in_proj",
    )(xs, mods, nw.reshape(1, D_MODEL), w_in, qw, kw, cos_t, sin_t, ones_bd)


def _pool_segment(u, w_bd, scale):
    n = u.shape[0]
    row = lax.broadcasted_iota(jnp.int32, u.shape, 0)
    grp = lax.broadcasted_iota(jnp.int32, u.shape, 1) // POOL_GROUP

    def prev(x, s):
        return jnp.where(row >= s, pltpu.roll(x, s, 0), 0.0)

    def nxt(x, s):
        return jnp.where(row < n - s, pltpu.roll(x, n - s, 0), 0.0)

    fwd = [u]
    bwd = [prev(u, 1)]
    for j in range(3):
        s = 1 << j
        fwd.append(fwd[j] + nxt(fwd[j], s))
        bwd.append(bwd[j] + prev(bwd[j], s))
    win = fwd[3] + bwd[3]
    for j in (2, 1, 0):
        win = jnp.where(grp == j, fwd[j] + bwd[j], win)
    half = jnp.left_shift(1, grp)
    cnt = jnp.minimum(row + half, n) - jnp.maximum(row - half, 0)
    y = win / cnt.astype(F32) - u
    return (_dot(y.astype(BF16), w_bd) * scale).astype(BF16)


def _pool_kernel(u_ref, w_ref, s_ref, o_ref):
    o_ref[0:CTX_LEN, :] = _pool_segment(u_ref[0:CTX_LEN, :], w_ref[...], s_ref[...])
    o_ref[CTX_LEN:T_ALL, :] = _pool_segment(u_ref[CTX_LEN:T_ALL, :], w_ref[...], s_ref[...])


def _pool_call(pool_u, w_bd, scale):
    nb = pool_u.shape[0]
    spec = pl.BlockSpec((None, T_ALL, POOL_WIDTH), lambda b: (b, 0, 0))
    return pl.pallas_call(
        _pool_kernel,
        out_shape=jax.ShapeDtypeStruct((nb, T_ALL, POOL_WIDTH), BF16),
        grid=(nb,),
        in_specs=[spec, _const_spec((POOL_WIDTH, POOL_WIDTH)), _const_spec((1, POOL_WIDTH))],
        out_specs=spec,
        compiler_params=_params(1),
        name="pool_mix",
    )(pool_u, w_bd, scale)


def _attn_kernel(q_ref, k_ref, v_ref, o_ref):
    def run(n_keys):
        q = q_ref[...].reshape(Q_PER_KV * TM, HEAD_DIM)
        s = _dot_nt(q, k_ref[0:n_keys, :])
        p = jnp.exp(s - jnp.max(s, axis=-1, keepdims=True))
        denom = jnp.sum(p, axis=-1, keepdims=True)
        o = _dot(p.astype(BF16), v_ref[0:n_keys, :]) / denom
        for j in range(Q_PER_KV):
            o_ref[:, j * HEAD_DIM:(j + 1) * HEAD_DIM] = o[j * TM:(j + 1) * TM, :].astype(BF16)

    is_ctx = pl.program_id(2) == 0

    @pl.when(is_ctx)
    def _():
        run(CTX_LEN)

    @pl.when(jnp.logical_not(is_ctx))
    def _():
        run(T_ALL)


def _attn_call(q, k, v):
    nb = q.shape[0]
    kv_spec = pl.BlockSpec((None, None, T_ALL, HEAD_DIM), lambda b, h, t: (b, h, 0, 0))
    return pl.pallas_call(
        _attn_kernel,
        out_shape=jax.ShapeDtypeStruct((nb, T_ALL, ATTN_WIDTH), BF16),
        grid=(nb, N_KV_HEADS, T_ALL // TM),
        in_specs=[pl.BlockSpec((None, Q_PER_KV, TM, HEAD_DIM), lambda b, h, t: (b, h, t, 0)),
                  kv_spec, kv_spec],
        out_specs=pl.BlockSpec((None, TM, Q_PER_KV * HEAD_DIM), lambda b, h, t: (b, t, h)),
        compiler_params=_params(3),
        name="attention",
    )(q, k, v)


HG_PREP_ROWS = 256


def _hgrn_gates(z, log_lb, log_1m_lb, one_m_lb):
    e = jnp.exp(-jnp.abs(z))
    log_sig = jnp.minimum(z, 0.0) - jnp.log1p(e)
    b = log_1m_lb + log_sig
    log_f = jnp.maximum(log_lb, b) + jnp.log1p(jnp.exp(-jnp.abs(log_lb - b)))
    one_m_f = one_m_lb * jnp.where(z >= 0, e, 1.0) / (1.0 + e)
    return log_f, one_m_f


def _hgrn_kernel(hg_ref, lb_ref, nw_ref, ones_ref, o_ref,
                 gf_ref, kf_ref, gb_ref, kb_ref, of_ref, ob_ref, sf_ref, sb_ref):
    C = HG_CHUNK
    W = HG_WIDTH

    def prep(i, carry):
        r0 = pl.multiple_of(i * HG_PREP_ROWS, HG_PREP_ROWS)
        rows = pl.ds(r0, HG_PREP_ROWS)
        pos = lax.broadcasted_iota(jnp.int32, (HG_PREP_ROWS, W), 0) & (C - 1)
        lf, kf = _hgrn_gates(hg_ref[rows, 2 * W:3 * W], lb_ref[0:1, :], lb_ref[1:2, :], lb_ref[2:3, :])
        lb_, kb = _hgrn_gates(hg_ref[rows, 3 * W:4 * W], lb_ref[3:4, :], lb_ref[4:5, :], lb_ref[5:6, :])
        for s in (1, 2, 4, 8):
            lf = lf + jnp.where(pos >= s, pltpu.roll(lf, s, 0), 0.0)
            lb_ = lb_ + jnp.where(pos < C - s, pltpu.roll(lb_, HG_PREP_ROWS - s, 0), 0.0)
        gf_ref[rows, :] = lf
        kf_ref[rows, :] = kf
        gb_ref[rows, :] = lb_
        kb_ref[rows, :] = kb
        return carry

    lax.fori_loop(0, T_ALL // HG_PREP_ROWS, prep, 0)

    sf_ref[...] = jnp.zeros((W, W), F32)
    sb_ref[...] = jnp.zeros((W, W), F32)
    ones_bd = ones_ref[...]
    same_head = (lax.broadcasted_iota(jnp.int32, (W, W), 0) // HG_DK
                 == lax.broadcasted_iota(jnp.int32, (W, W), 1) // HG_DK)
    tpos = lax.broadcasted_iota(jnp.int32, (C, W), 0)

    def chunk(n, g_ref, k_ref, s_ref, out_ref, forward):
        rows = pl.ds(pl.multiple_of(n * C, C), C)
        g = g_ref[rows, :]
        kk = k_ref[rows, :]
        q = hg_ref[rows, 0:W]
        v = hg_ref[rows, W:2 * W]
        g_end = g[C - 1:C, :] if forward else g[0:1, :]
        st = s_ref[...]
        o = _dot_nt((q * jnp.exp(g)).astype(BF16), st.astype(BF16))
        parts = []
        for s in range(C):
            causal = (tpos >= s) if forward else (tpos <= s)
            decay = jnp.where(causal, jnp.exp(g - g[s:s + 1, :]), 0.0)
            parts.append(q * kk[s:s + 1, :] * decay)
        a = _dot(jnp.concatenate(parts, axis=0).astype(BF16), ones_bd)
        for s in range(C):
            o = o + a[s * C:(s + 1) * C, :] * v[s:s + 1, :]
        out_ref[rows, :] = o
        k_dec = kk * jnp.exp(g_end - g)
        kv_t = lax.dot_general(v.astype(BF16), k_dec.astype(BF16), (((0,), (0,)), ((), ())),
                               preferred_element_type=F32)
        s_ref[...] = st * jnp.exp(g_end) + jnp.where(same_head, kv_t, 0.0)

    def step(i, carry):
        chunk(i, gf_ref, kf_ref, sf_ref, of_ref, True)
        nbk = jnp.where(i < CTX_CHUNKS, CTX_CHUNKS - 1 - i, N_CHUNKS - 1 + CTX_CHUNKS - i)
        chunk(nbk, gb_ref, kb_ref, sb_ref, ob_ref, False)
        return carry

    lax.fori_loop(0, N_CHUNKS, step, 0)

    def readout(i, carry):
        r0 = pl.multiple_of(i * HG_PREP_ROWS, HG_PREP_ROWS)
        rows = pl.ds(r0, HG_PREP_ROWS)
        o = of_ref[rows, :] + ob_ref[rows, :]
        y = o * lax.rsqrt(_head_mean_sq(o, ones_bd) + NORM_EPS) * nw_ref[...]
        gate = hg_ref[rows, 4 * W:5 * W]
        o_ref[rows, :] = (y * (gate * _sigmoid(gate))).astype(BF16)
        return carry

    lax.fori_loop(0, T_ALL // HG_PREP_ROWS, readout, 0)


def _hgrn_call(hg5, lb_tab, nw, ones_bd):
    nb = hg5.shape[0]
    seq = lambda: pltpu.VMEM((T_ALL, HG_WIDTH), F32)
    return pl.pallas_call(
        _hgrn_kernel,
        out_shape=jax.ShapeDtypeStruct((nb, T_ALL, HG_WIDTH), BF16),
        grid=(nb,),
        in_specs=[pl.BlockSpec((None, T_ALL, 5 * HG_WIDTH), lambda b: (b, 0, 0),
                               pipeline_mode=pl.Buffered(1)),
                  _const_spec((6, HG_WIDTH)), _const_spec((1, HG_WIDTH)),
                  _const_spec((HG_WIDTH, HG_WIDTH))],
        out_specs=pl.BlockSpec((None, T_ALL, HG_WIDTH), lambda b: (b, 0, 0)),
        scratch_shapes=[seq(), seq(), seq(), seq(), seq(), seq(),
                        pltpu.VMEM((HG_WIDTH, HG_WIDTH), F32), pltpu.VMEM((HG_WIDTH, HG_WIDTH), F32)],
        compiler_params=_params(1),
        name="hgrn2",
    )(hg5, lb_tab, nw, ones_bd)


def _merge_kernel(pool_ref, attn_ref, hgo_ref, gate_ref, x_ref, mod_ref,
                  wbp_ref, wba_ref, wbh_ref, wo_ref, o_ref):
    y = (gate_ref[:, 0:D_MODEL].astype(F32) * _dot(pool_ref[...], wbp_ref[...])
         + gate_ref[:, D_MODEL:2 * D_MODEL].astype(F32) * _dot(attn_ref[...], wba_ref[...])
         + gate_ref[:, 2 * D_MODEL:3 * D_MODEL].astype(F32) * _dot(hgo_ref[...], wbh_ref[...]))
    z = _dot(y.astype(BF16), wo_ref[...])
    o_ref[...] = x_ref[...] + mod_ref[2:3, :] * z


def _merge_call(pool_o, attn_o, hg_o, gates, xs, mods, wbp, wba, wbh, wo, l):
    nb = xs.shape[0]
    return pl.pallas_call(
        _merge_kernel,
        out_shape=jax.ShapeDtypeStruct((nb, T_ALL, D_MODEL), F32),
        grid=(nb, T_ALL // TM),
        in_specs=[_tok_spec(POOL_WIDTH), _tok_spec(ATTN_WIDTH), _tok_spec(HG_WIDTH),
                  _tok_spec(3 * D_MODEL), _tok_spec(D_MODEL), _mod_spec(l, nb),
                  _const_spec((POOL_WIDTH, D_MODEL)), _const_spec((ATTN_WIDTH, D_MODEL)),
                  _const_spec((HG_WIDTH, D_MODEL)), _const_spec((D_MODEL, D_MODEL))],
        out_specs=_tok_spec(D_MODEL),
        compiler_params=_params(2),
        name="merge",
    )(pool_o, attn_o, hg_o, gates, xs, mods, wbp, wba, wbh, wo)


def _ffn_kernel(x_ref, mod_ref, nw_ref, w1_ref, w2_ref, o_ref):
    x = x_ref[...]
    h = _rms_mod(x, nw_ref[...], mod_ref[4:5, :], mod_ref[3:4, :]).astype(BF16)
    a = _dot(h, w1_ref[:, 0:D_FF])
    b = _dot(h, w1_ref[:, D_FF:2 * D_FF])
    s = (a * _sigmoid(a) * b).astype(BF16)
    o_ref[...] = x + mod_ref[5:6, :] * _dot(s, w2_ref[...])


def _ffn_call(xs, mods, nw, w1, w2, l):
    nb = xs.shape[0]
    return pl.pallas_call(
        _ffn_kernel,
        out_shape=jax.ShapeDtypeStruct((nb, T_ALL, D_MODEL), F32),
        grid=(nb, T_ALL // TM),
        in_specs=[_tok_spec(D_MODEL), _mod_spec(l, nb), _const_spec((1, D_MODEL)),
                  _const_spec((D_MODEL, 2 * D_FF)), _const_spec((D_FF, D_MODEL))],
        out_specs=_tok_spec(D_MODEL),
        compiler_params=_params(2),
        name="ffn",
    )(xs, mods, nw.reshape(1, D_MODEL), w1, w2)


def _rope_tables():
    t = jnp.arange(SEQ, dtype=jnp.int32)
    freqs = ROPE_THETA ** (-jnp.arange(ROPE_HALF, dtype=F32) * (2.0 / (2 * ROPE_HALF)))
    cos_parts, sin_parts = [], []
    for pos in (t // GRID_W, t % GRID_W):
        ang = pos.astype(F32)[:, None] * freqs[None, :]
        cos_parts += [jnp.cos(ang), jnp.cos(ang)]
        sin_parts += [-jnp.sin(ang), jnp.sin(ang)]
    cos = jnp.concatenate(cos_parts, axis=1)
    sin = jnp.concatenate(sin_parts, axis=1)
    cos = jnp.concatenate([jnp.ones((CTX_LEN, HEAD_DIM), F32), cos], axis=0)
    sin = jnp.concatenate([jnp.zeros((CTX_LEN, HEAD_DIM), F32), sin], axis=0)
    return jnp.tile(cos, (1, N_KV_HEADS)), jnp.tile(sin, (1, N_KV_HEADS))


def _block_diag_ones(n):
    i = jnp.arange(n) // HEAD_DIM
    return (i[:, None] == i[None, :]).astype(BF16)


def _pool_block_diag(pool_w_l):
    w = jnp.zeros((POOL_WIDTH, POOL_WIDTH), F32)
    for g in range(POOL_WIDTH // POOL_GROUP):
        sl = slice(g * POOL_GROUP, (g + 1) * POOL_GROUP)
        w = w.at[sl, sl].set(pool_w_l[g])
    return w.astype(BF16)


def kernel(x, c, ctx, c_ctx, w_ada, b_ada, norm1_w, w_in, pool_w, pool_scale, q_norm_w, k_norm_w,
           hg_lb_logits, hg_norm_w, w_branch_pool, w_branch_attn, w_branch_hg, w_out, norm2_w,
           w_ffn_in, w_ffn_out):
    lb_cum = jnp.cumsum(jax.nn.softmax(hg_lb_logits.astype(F32), axis=0), axis=0)
    lb_all = lb_cum - lb_cum[0]
    lb_tab = jnp.stack([jnp.log(lb_all), jnp.log1p(-lb_all), 1.0 - lb_all], axis=2)
    lb_tab = lb_tab.reshape(DEPTH, 6, HG_WIDTH)

    cos_t, sin_t = _rope_tables()
    ones_bd = _block_diag_ones(ATTN_WIDTH)
    ones_hg = ones_bd[:HG_WIDTH, :HG_WIDTH]

    cc = jnp.zeros((MOD_ROWS, D_MODEL), F32).at[:BATCH].set(c).at[BATCH].set(c_ctx)
    mods = _ada_call(cc, w_ada, b_ada).reshape(DEPTH, MOD_ROWS, N_MOD, D_MODEL)

    xs = jnp.concatenate([ctx, x], axis=1)
    for l in range(DEPTH):
        pool_u, q, k, v, hg5, gates = _inproj_call(
            xs, mods, norm1_w[l], w_in[l].astype(BF16),
            jnp.tile(q_norm_w[l], N_HEADS).reshape(1, ATTN_WIDTH),
            jnp.tile(k_norm_w[l], N_KV_HEADS).reshape(1, KV_WIDTH), cos_t, sin_t, ones_bd, l)
        pool_o = _pool_call(pool_u, _pool_block_diag(pool_w[l]), pool_scale[l].reshape(1, POOL_WIDTH))
        attn_o = _attn_call(q, k, v)
        hg_o = _hgrn_call(hg5, lb_tab[l], jnp.tile(hg_norm_w[l], HG_HEADS).reshape(1, HG_WIDTH), ones_hg)
        xs = _merge_call(pool_o, attn_o, hg_o, gates, xs, mods,
                         w_branch_pool[l].astype(BF16), w_branch_attn[l].astype(BF16),
                         w_branch_hg[l].astype(BF16), w_out[l].astype(BF16), l)
        xs = _ffn_call(xs, mods, norm2_w[l], w_ffn_in[l].astype(BF16), w_ffn_out[l].astype(BF16), l)
    return xs[:, CTX_LEN:]
```

```python
import jax
import jax.numpy as jnp
from jax import lax
from jax.experimental import pallas as pl
from jax.experimental.pallas import tpu as pltpu

D_MODEL = 1024
BATCH = 8
SEQ = 2048
DEPTH = 4
GRID_W = 64
CTX_LEN = 256
T_ALL = CTX_LEN + SEQ
NORM_EPS = 1e-6
N_MOD = 6

POOL_GROUP = 64
POOL_WIDTH = 256
N_HEADS = 8
N_KV_HEADS = 2
HEAD_DIM = 64
Q_PER_KV = N_HEADS // N_KV_HEADS
ATTN_WIDTH = N_HEADS * HEAD_DIM
KV_WIDTH = N_KV_HEADS * HEAD_DIM
ROPE_THETA = 10000.0
ROPE_HALF = HEAD_DIM // 4
LOG2_E = 1.4426950408889634
HG_HEADS = 4
HG_DK = 64
HG_WIDTH = 256
HG_CHUNK = 16
N_CHUNKS = T_ALL // HG_CHUNK
CTX_CHUNKS = CTX_LEN // HG_CHUNK
D_FF = 2816
IN_WIDTH = 5376

C_POOL = 0
C_QKV = 256
C_HG = 1024
C_GATE = 2304

MOD_ROWS = 16
TM = 256
VMEM_LIMIT = 56 * 1024 * 1024

BF16 = jnp.bfloat16
F32 = jnp.float32


def _dot(a, b):
    return jnp.dot(a, b, preferred_element_type=F32)


def _dot_nt(a, b):
    return lax.dot_general(a, b, (((1,), (1,)), ((), ())), preferred_element_type=F32)


def _sigmoid(x):
    return 1.0 / (1.0 + jnp.exp(-x))


def _params(n_grid):
    return pltpu.CompilerParams(dimension_semantics=("arbitrary",) * n_grid,
                                vmem_limit_bytes=VMEM_LIMIT)


def _ada_kernel(c_ref, w_ref, b_ref, o_ref):
    c = c_ref[...]
    act = (c * _sigmoid(c)).astype(BF16)
    o_ref[...] = _dot(act, w_ref[...].astype(BF16)) + b_ref[...]


def _ada_call(cc, w_ada, b_ada):
    tn = 1536
    n = N_MOD * D_MODEL
    return pl.pallas_call(
        _ada_kernel,
        out_shape=jax.ShapeDtypeStruct((DEPTH, MOD_ROWS, n), F32),
        grid=(DEPTH, n // tn),
        in_specs=[
            pl.BlockSpec((MOD_ROWS, D_MODEL), lambda l, j: (0, 0)),
            pl.BlockSpec((None, D_MODEL, tn), lambda l, j: (l, 0, j)),
            pl.BlockSpec((None, 1, tn), lambda l, j: (l, 0, j)),
        ],
        out_specs=pl.BlockSpec((None, MOD_ROWS, tn), lambda l, j: (l, 0, j)),
        compiler_params=_params(2),
        name="ada_mod",
    )(cc, w_ada, b_ada.reshape(DEPTH, 1, n))


def _mod_spec(l, nb):
    return pl.BlockSpec((None, None, N_MOD, D_MODEL),
                        lambda b, t: (l, jnp.where(t == 0, BATCH, b), 0, 0))


def _const_spec(shape):
    nd = len(shape)
    return pl.BlockSpec(shape, lambda *_: (0,) * nd, pipeline_mode=pl.Buffered(1))


def _tok_spec(width):
    return pl.BlockSpec((None, TM, width), lambda b, t: (b, t, 0))


def _rms_mod(x, nw, scale, shift):
    ms = jnp.mean(x * x, axis=-1, keepdims=True)
    y = x * lax.rsqrt(ms + NORM_EPS) * nw
    return y * (1.0 + scale) + shift


def _head_mean_sq(x, ones_bd):
    x2 = x * x
    hi = x2.astype(BF16)
    lo = (x2 - hi.astype(F32)).astype(BF16)
    return (_dot(hi, ones_bd) + _dot(lo, ones_bd)) * (1.0 / HEAD_DIM)


def _swap_rope_halves(x):
    w = x.shape[1]
    lane = lax.broadcasted_iota(jnp.int32, x.shape, 1)
    up = pltpu.roll(x, w - ROPE_HALF, 1)
    dn = pltpu.roll(x, ROPE_HALF, 1)
    return jnp.where((lane & ROPE_HALF) == 0, up, dn)


def _inproj_kernel(x_ref, mod_ref, nw_ref, w_ref, qw_ref, kw_ref, cos_ref, sin_ref, ones_ref,
                   pool_ref, q_ref, k_ref, v_ref, hg_ref, gate_ref):
    h = _rms_mod(x_ref[...], nw_ref[...], mod_ref[1:2, :], mod_ref[0:1, :]).astype(BF16)
    pool_ref[...] = _dot(h, w_ref[:, C_POOL:C_QKV])

    qkv = _dot(h, w_ref[:, C_QKV:C_HG])
    cos = cos_ref[...]
    sin = sin_ref[...]
    q = qkv[:, 0:ATTN_WIDTH]
    q = q * lax.rsqrt(_head_mean_sq(q, ones_ref[...]) + NORM_EPS) * qw_ref[...]
    cos_q = jnp.concatenate([cos] * (ATTN_WIDTH // KV_WIDTH), axis=1)
    sin_q = jnp.concatenate([sin] * (ATTN_WIDTH // KV_WIDTH), axis=1)
    q = (q * cos_q + _swap_rope_halves(q) * sin_q) * (HEAD_DIM ** -0.5 * LOG2_E)
    for hh in range(N_HEADS):
        q_ref[hh] = q[:, hh * HEAD_DIM:(hh + 1) * HEAD_DIM].astype(BF16)
    k = qkv[:, ATTN_WIDTH:ATTN_WIDTH + KV_WIDTH]
    k = k * lax.rsqrt(_head_mean_sq(k, ones_ref[0:KV_WIDTH, 0:KV_WIDTH]) + NORM_EPS) * kw_ref[...]
    k = k * cos + _swap_rope_halves(k) * sin
    v = qkv[:, ATTN_WIDTH + KV_WIDTH:]
    for hh in range(N_KV_HEADS):
        k_ref[hh] = k[:, hh * HEAD_DIM:(hh + 1) * HEAD_DIM].astype(BF16)
        v_ref[hh] = v[:, hh * HEAD_DIM:(hh + 1) * HEAD_DIM].astype(BF16)

    hg_ref[...] = _dot(h, w_ref[:, C_HG:C_GATE])
    gate_ref[...] = _sigmoid(_dot(h, w_ref[:, C_GATE:IN_WIDTH])).astype(BF16)


def _inproj_call(xs, mods, nw, w_in, qw, kw, cos_t, sin_t, ones_bd, l):
    nb = xs.shape[0]
    head_spec = lambda nh: pl.BlockSpec((None, nh, TM, HEAD_DIM), lambda b, t: (b, 0, t, 0))
    rope_spec = pl.BlockSpec((TM, KV_WIDTH), lambda b, t: (t, 0))
    out_shape = [
        jax.ShapeDtypeStruct((nb, T_ALL, POOL_WIDTH), F32),
        jax.ShapeDtypeStruct((nb, N_HEADS, T_ALL, HEAD_DIM), BF16),
        jax.ShapeDtypeStruct((nb, N_KV_HEADS, T_ALL, HEAD_DIM), BF16),
        jax.ShapeDtypeStruct((nb, N_KV_HEADS, T_ALL, HEAD_DIM), BF16),
        jax.ShapeDtypeStruct((nb, T_ALL, C_GATE - C_HG), F32),
        jax.ShapeDtypeStruct((nb, T_ALL, IN_WIDTH - C_GATE), BF16),
    ]
    out_specs = [_tok_spec(POOL_WIDTH), head_spec(N_HEADS), head_spec(N_KV_HEADS),
                 head_spec(N_KV_HEADS), _tok_spec(C_GATE - C_HG), _tok_spec(IN_WIDTH - C_GATE)]
    return pl.pallas_call(
        _inproj_kernel,
        out_shape=out_shape,
        grid=(nb, T_ALL // TM),
        in_specs=[_tok_spec(D_MODEL), _mod_spec(l, nb), _const_spec((1, D_MODEL)),
                  _const_spec((D_MODEL, IN_WIDTH)), _const_spec((1, ATTN_WIDTH)),
                  _const_spec((1, KV_WIDTH)), rope_spec, rope_spec,
                  _const_spec((ATTN_WIDTH, ATTN_WIDTH))],
        out_specs=out_specs,
        compiler_params=_params(2),
        name="in_proj",
    )(xs, mods, nw.reshape(1, D_MODEL), w_in, qw, kw, cos_t, sin_t, ones_bd)


def _pool_segment(u, w_bd, scale):
    n = u.shape[0]
    row = lax.broadcasted_iota(jnp.int32, u.shape, 0)
    grp = lax.broadcasted_iota(jnp.int32, u.shape, 1) // POOL_GROUP

    def prev(x, s):
        return jnp.where(row >= s, pltpu.roll(x, s, 0), 0.0)

    def nxt(x, s):
        return jnp.where(row < n - s, pltpu.roll(x, n - s, 0), 0.0)

    fwd = [u]
    bwd = [prev(u, 1)]
    for j in range(3):
        s = 1 << j
        fwd.append(fwd[j] + nxt(fwd[j], s))
        bwd.append(bwd[j] + prev(bwd[j], s))
    win = fwd[3] + bwd[3]
    for j in (2, 1, 0):
        win = jnp.where(grp == j, fwd[j] + bwd[j], win)
    half = jnp.left_shift(1, grp)
    cnt = jnp.minimum(row + half, n) - jnp.maximum(row - half, 0)
    y = win / cnt.astype(F32) - u
    return (_dot(y.astype(BF16), w_bd) * scale).astype(BF16)


def _pool_kernel(u_ref, w_ref, s_ref, o_ref):
    o_ref[0:CTX_LEN, :] = _pool_segment(u_ref[0:CTX_LEN, :], w_ref[...], s_ref[...])
    o_ref[CTX_LEN:T_ALL, :] = _pool_segment(u_ref[CTX_LEN:T_ALL, :], w_ref[...], s_ref[...])


def _pool_call(pool_u, w_bd, scale):
    nb = pool_u.shape[0]
    spec = pl.BlockSpec((None, T_ALL, POOL_WIDTH), lambda b: (b, 0, 0))
    return pl.pallas_call(
        _pool_kernel,
        out_shape=jax.ShapeDtypeStruct((nb, T_ALL, POOL_WIDTH), BF16),
        grid=(nb,),
        in_specs=[spec, _const_spec((POOL_WIDTH, POOL_WIDTH)), _const_spec((1, POOL_WIDTH))],
        out_specs=spec,
        compiler_params=_params(1),
        name="pool_mix",
    )(pool_u, w_bd, scale)


ATTN_ROWS = 128


def _attn_kernel(q_ref, k_ref, v_ref, o_ref):
    def run(n_keys):
        k = k_ref[0:n_keys, :]
        v = v_ref[0:n_keys, :]
        blocks = [(j, slice(r * ATTN_ROWS, (r + 1) * ATTN_ROWS))
                  for j in range(Q_PER_KV) for r in range(TM // ATTN_ROWS)]
        scores = lambda blk: _dot_nt(q_ref[blk[0], blk[1], :], k)
        s_next = scores(blocks[0])
        for i, (j, rows) in enumerate(blocks):
            s = s_next
            if i + 1 < len(blocks):
                s_next = scores(blocks[i + 1])
            p = jnp.exp2(s - jnp.max(s, axis=-1, keepdims=True))
            denom = jnp.sum(p, axis=-1, keepdims=True)
            o = _dot(p.astype(BF16), v) / denom
            o_ref[rows, j * HEAD_DIM:(j + 1) * HEAD_DIM] = o.astype(BF16)

    is_ctx = pl.program_id(2) == 0

    @pl.when(is_ctx)
    def _():
        run(CTX_LEN)

    @pl.when(jnp.logical_not(is_ctx))
    def _():
        run(T_ALL)


def _attn_call(q, k, v):
    nb = q.shape[0]
    kv_spec = pl.BlockSpec((None, None, T_ALL, HEAD_DIM), lambda b, h, t: (b, h, 0, 0))
    return pl.pallas_call(
        _attn_kernel,
        out_shape=jax.ShapeDtypeStruct((nb, T_ALL, ATTN_WIDTH), BF16),
        grid=(nb, N_KV_HEADS, T_ALL // TM),
        in_specs=[pl.BlockSpec((None, Q_PER_KV, TM, HEAD_DIM), lambda b, h, t: (b, h, t, 0)),
                  kv_spec, kv_spec],
        out_specs=pl.BlockSpec((None, TM, Q_PER_KV * HEAD_DIM), lambda b, h, t: (b, t, h)),
        compiler_params=_params(3),
        name="attention",
    )(q, k, v)


HG_PREP_ROWS = 256
HG_UNROLL = 8


def _hgrn_gates(z, log_lb, log_1m_lb):
    log_sig = jnp.minimum(z, 0.0) - jnp.log(1.0 + jnp.exp(-jnp.abs(z)))
    b = log_1m_lb + log_sig
    log_f = jnp.maximum(log_lb, b) + jnp.log(1.0 + jnp.exp(-jnp.abs(log_lb - b)))
    return log_f, b - z


def _hgrn_kernel(hg_ref, lb_ref, nw_ref, ones_ref, o_ref,
                 gf_ref, hf_ref, gb_ref, hb_ref, of_ref, ob_ref, sf_ref, sb_ref):
    C = HG_CHUNK
    W = HG_WIDTH

    def prep(i, carry):
        r0 = pl.multiple_of(i * HG_PREP_ROWS, HG_PREP_ROWS)
        rows = pl.ds(r0, HG_PREP_ROWS)
        pos = lax.broadcasted_iota(jnp.int32, (HG_PREP_ROWS, W), 0) & (C - 1)
        lf, lkf = _hgrn_gates(hg_ref[rows, 2 * W:3 * W], lb_ref[0:1, :], lb_ref[1:2, :])
        lb_, lkb = _hgrn_gates(hg_ref[rows, 3 * W:4 * W], lb_ref[2:3, :], lb_ref[3:4, :])
        for s in (1, 2, 4, 8):
            lf = lf + jnp.where(pos >= s, pltpu.roll(lf, s, 0), 0.0)
            lb_ = lb_ + jnp.where(pos < C - s, pltpu.roll(lb_, HG_PREP_ROWS - s, 0), 0.0)
        gf_ref[rows, :] = lf * LOG2_E
        hf_ref[rows, :] = (lf - lkf) * LOG2_E
        gb_ref[rows, :] = lb_ * LOG2_E
        hb_ref[rows, :] = (lb_ - lkb) * LOG2_E
        return carry

    lax.fori_loop(0, T_ALL // HG_PREP_ROWS, prep, 0)

    sf_ref[...] = jnp.zeros((HG_DK, W), F32)
    sb_ref[...] = jnp.zeros((HG_DK, W), F32)
    ones_bd = ones_ref[...]
    SUB = 8
    tpos = lax.broadcasted_iota(jnp.int32, (SUB, W), 0)
    lane_head = lax.broadcasted_iota(jnp.int32, (C, W), 1) // HG_DK

    def by_head(x):
        return jnp.concatenate([jnp.where(lane_head == hh, x, 0.0) for hh in range(HG_HEADS)], axis=0)

    def chunk_local(n, g_ref, h_ref, forward):
        rows = pl.ds(pl.multiple_of(n * C, C), C)
        g = g_ref[rows, :]
        h = h_ref[rows, :]
        q = hg_ref[rows, 0:W]
        v = hg_ref[rows, W:2 * W]
        g_end = g[C - 1:C, :] if forward else g[0:1, :]
        q_dec = by_head(q * jnp.exp2(g)).astype(BF16)
        parts, where = [], []
        for s in range(C):
            hs = h[s:s + 1, :]
            for t0 in range(0, C, SUB):
                if (forward and t0 + SUB - 1 < s) or (not forward and t0 > s):
                    continue
                decay = jnp.exp2(g[t0:t0 + SUB, :] - hs)
                if t0 <= s < t0 + SUB:
                    causal = (tpos >= s - t0) if forward else (tpos <= s - t0)
                    decay = jnp.where(causal, decay, 0.0)
                parts.append(q[t0:t0 + SUB, :] * decay)
                where.append((s, t0))
        a = _dot(jnp.concatenate(parts, axis=0).astype(BF16), ones_bd)
        o_tiles = {t0: None for t0 in range(0, C, SUB)}
        for i, (s, t0) in enumerate(where):
            term = a[i * SUB:(i + 1) * SUB, :] * v[s:s + 1, :]
            o_tiles[t0] = term if o_tiles[t0] is None else o_tiles[t0] + term
        o_intra = jnp.concatenate([o_tiles[t0] for t0 in range(0, C, SUB)], axis=0)
        k_dec = by_head(jnp.exp2(g_end - h)).astype(BF16)
        v_rows = jnp.concatenate([v[:, hh * HG_DK:(hh + 1) * HG_DK] for hh in range(HG_HEADS)], axis=0)
        kv = lax.dot_general(v_rows.astype(BF16), k_dec, (((0,), (0,)), ((), ())),
                             preferred_element_type=F32)
        return rows, q_dec, o_intra, jnp.exp2(g_end), kv

    def chunk_carry(st, local, out_ref):
        rows, q_dec, o_intra, a_end, kv = local
        r = _dot_nt(q_dec, st.astype(BF16))
        o_inter = jnp.concatenate([r[hh * C:(hh + 1) * C, :] for hh in range(HG_HEADS)], axis=1)
        out_ref[rows, :] = o_inter + o_intra
        return st * a_end + kv

    def step(i, carry):
        local_f, local_b = [], []
        for u in range(HG_UNROLL):
            n = i * HG_UNROLL + u
            local_f.append(chunk_local(n, gf_ref, hf_ref, True))
            nbk = jnp.where(n < CTX_CHUNKS, CTX_CHUNKS - 1 - n, N_CHUNKS - 1 + CTX_CHUNKS - n)
            local_b.append(chunk_local(nbk, gb_ref, hb_ref, False))
        st_f = sf_ref[...]
        st_b = sb_ref[...]
        for u in range(HG_UNROLL):
            st_f = chunk_carry(st_f, local_f[u], of_ref)
            st_b = chunk_carry(st_b, local_b[u], ob_ref)
        sf_ref[...] = st_f
        sb_ref[...] = st_b
        return carry

    lax.fori_loop(0, N_CHUNKS // HG_UNROLL, step, 0)

    def readout(i, carry):
        r0 = pl.multiple_of(i * HG_PREP_ROWS, HG_PREP_ROWS)
        rows = pl.ds(r0, HG_PREP_ROWS)
        o = of_ref[rows, :] + ob_ref[rows, :]
        y = o * lax.rsqrt(_head_mean_sq(o, ones_bd) + NORM_EPS) * nw_ref[...]
        gate = hg_ref[rows, 4 * W:5 * W]
        o_ref[rows, :] = (y * (gate * _sigmoid(gate))).astype(BF16)
        return carry

    lax.fori_loop(0, T_ALL // HG_PREP_ROWS, readout, 0)


def _hgrn_call(hg5, lb_tab, nw, ones_bd):
    nb = hg5.shape[0]
    seq = lambda: pltpu.VMEM((T_ALL, HG_WIDTH), F32)
    return pl.pallas_call(
        _hgrn_kernel,
        out_shape=jax.ShapeDtypeStruct((nb, T_ALL, HG_WIDTH), BF16),
        grid=(nb,),
        in_specs=[pl.BlockSpec((None, T_ALL, 5 * HG_WIDTH), lambda b: (b, 0, 0),
                               pipeline_mode=pl.Buffered(1)),
                  _const_spec((4, HG_WIDTH)), _const_spec((1, HG_WIDTH)),
                  _const_spec((HG_WIDTH, HG_WIDTH))],
        out_specs=pl.BlockSpec((None, T_ALL, HG_WIDTH), lambda b: (b, 0, 0)),
        scratch_shapes=[seq(), seq(), seq(), seq(), seq(), seq(),
                        pltpu.VMEM((HG_DK, HG_WIDTH), F32), pltpu.VMEM((HG_DK, HG_WIDTH), F32)],
        compiler_params=_params(1),
        name="hgrn2",
    )(hg5, lb_tab, nw, ones_bd)


def _merge_kernel(pool_ref, attn_ref, hgo_ref, gate_ref, x_ref, mod_ref,
                  wbp_ref, wba_ref, wbh_ref, wo_ref, o_ref):
    y = (gate_ref[:, 0:D_MODEL].astype(F32) * _dot(pool_ref[...], wbp_ref[...])
         + gate_ref[:, D_MODEL:2 * D_MODEL].astype(F32) * _dot(attn_ref[...], wba_ref[...])
         + gate_ref[:, 2 * D_MODEL:3 * D_MODEL].astype(F32) * _dot(hgo_ref[...], wbh_ref[...]))
    z = _dot(y.astype(BF16), wo_ref[...])
    o_ref[...] = x_ref[...] + mod_ref[2:3, :] * z


def _merge_call(pool_o, attn_o, hg_o, gates, xs, mods, wbp, wba, wbh, wo, l):
    nb = xs.shape[0]
    return pl.pallas_call(
        _merge_kernel,
        out_shape=jax.ShapeDtypeStruct((nb, T_ALL, D_MODEL), F32),
        grid=(nb, T_ALL // TM),
        in_specs=[_tok_spec(POOL_WIDTH), _tok_spec(ATTN_WIDTH), _tok_spec(HG_WIDTH),
                  _tok_spec(3 * D_MODEL), _tok_spec(D_MODEL), _mod_spec(l, nb),
                  _const_spec((POOL_WIDTH, D_MODEL)), _const_spec((ATTN_WIDTH, D_MODEL)),
                  _const_spec((HG_WIDTH, D_MODEL)), _const_spec((D_MODEL, D_MODEL))],
        out_specs=_tok_spec(D_MODEL),
        compiler_params=_params(2),
        name="merge",
    )(pool_o, attn_o, hg_o, gates, xs, mods, wbp, wba, wbh, wo)


def _ffn_kernel(x_ref, mod_ref, nw_ref, w1_ref, w2_ref, o_ref):
    x = x_ref[...]
    h = _rms_mod(x, nw_ref[...], mod_ref[4:5, :], mod_ref[3:4, :]).astype(BF16)
    a = _dot(h, w1_ref[:, 0:D_FF])
    b = _dot(h, w1_ref[:, D_FF:2 * D_FF])
    s = (a * _sigmoid(a) * b).astype(BF16)
    o_ref[...] = x + mod_ref[5:6, :] * _dot(s, w2_ref[...])


def _ffn_call(xs, mods, nw, w1, w2, l):
    nb = xs.shape[0]
    return pl.pallas_call(
        _ffn_kernel,
        out_shape=jax.ShapeDtypeStruct((nb, T_ALL, D_MODEL), F32),
        grid=(nb, T_ALL // TM),
        in_specs=[_tok_spec(D_MODEL), _mod_spec(l, nb), _const_spec((1, D_MODEL)),
                  _const_spec((D_MODEL, 2 * D_FF)), _const_spec((D_FF, D_MODEL))],
        out_specs=_tok_spec(D_MODEL),
        compiler_params=_params(2),
        name="ffn",
    )(xs, mods, nw.reshape(1, D_MODEL), w1, w2)


def _rope_tables():
    t = jnp.arange(SEQ, dtype=jnp.int32)
    freqs = ROPE_THETA ** (-jnp.arange(ROPE_HALF, dtype=F32) * (2.0 / (2 * ROPE_HALF)))
    cos_parts, sin_parts = [], []
    for pos in (t // GRID_W, t % GRID_W):
        ang = pos.astype(F32)[:, None] * freqs[None, :]
        cos_parts += [jnp.cos(ang), jnp.cos(ang)]
        sin_parts += [-jnp.sin(ang), jnp.sin(ang)]
    cos = jnp.concatenate(cos_parts, axis=1)
    sin = jnp.concatenate(sin_parts, axis=1)
    cos = jnp.concatenate([jnp.ones((CTX_LEN, HEAD_DIM), F32), cos], axis=0)
    sin = jnp.concatenate([jnp.zeros((CTX_LEN, HEAD_DIM), F32), sin], axis=0)
    return jnp.tile(cos, (1, N_KV_HEADS)), jnp.tile(sin, (1, N_KV_HEADS))


def _block_diag_ones(n):
    i = jnp.arange(n) // HEAD_DIM
    return (i[:, None] == i[None, :]).astype(BF16)


def _pool_block_diag(pool_w_l):
    w = jnp.zeros((POOL_WIDTH, POOL_WIDTH), F32)
    for g in range(POOL_WIDTH // POOL_GROUP):
        sl = slice(g * POOL_GROUP, (g + 1) * POOL_GROUP)
        w = w.at[sl, sl].set(pool_w_l[g])
    return w.astype(BF16)


def kernel(x, c, ctx, c_ctx, w_ada, b_ada, norm1_w, w_in, pool_w, pool_scale, q_norm_w, k_norm_w,
           hg_lb_logits, hg_norm_w, w_branch_pool, w_branch_attn, w_branch_hg, w_out, norm2_w,
           w_ffn_in, w_ffn_out):
    lb_cum = jnp.cumsum(jax.nn.softmax(hg_lb_logits.astype(F32), axis=0), axis=0)
    lb_all = lb_cum - lb_cum[0]
    lb_tab = jnp.stack([jnp.log(lb_all), jnp.log1p(-lb_all)], axis=2)
    lb_tab = lb_tab.reshape(DEPTH, 4, HG_WIDTH)

    cos_t, sin_t = _rope_tables()
    ones_bd = _block_diag_ones(ATTN_WIDTH)
    ones_hg = ones_bd[:HG_WIDTH, :HG_WIDTH]

    cc = jnp.zeros((MOD_ROWS, D_MODEL), F32).at[:BATCH].set(c).at[BATCH].set(c_ctx)
    mods = _ada_call(cc, w_ada, b_ada).reshape(DEPTH, MOD_ROWS, N_MOD, D_MODEL)

    xs = jnp.concatenate([ctx, x], axis=1)
    for l in range(DEPTH):
        pool_u, q, k, v, hg5, gates = _inproj_call(
            xs, mods, norm1_w[l], w_in[l].astype(BF16),
            jnp.tile(q_norm_w[l], N_HEADS).reshape(1, ATTN_WIDTH),
            jnp.tile(k_norm_w[l], N_KV_HEADS).reshape(1, KV_WIDTH), cos_t, sin_t, ones_bd, l)
        pool_o = _pool_call(pool_u, _pool_block_diag(pool_w[l]), pool_scale[l].reshape(1, POOL_WIDTH))
        attn_o = _attn_call(q, k, v)
        hg_o = _hgrn_call(hg5, lb_tab[l], jnp.tile(hg_norm_w[l], HG_HEADS).reshape(1, HG_WIDTH), ones_hg)
        xs = _merge_call(pool_o, attn_o, hg_o, gates, xs, mods,
                         w_branch_pool[l].astype(BF16), w_branch_attn[l].astype(BF16),
                         w_branch_hg[l].astype(BF16), w_out[l].astype(BF16), l)
        xs = _ffn_call(xs, mods, norm2_w[l], w_ffn_in[l].astype(BF16), w_ffn_out[l].astype(BF16), l)
    return xs[:, CTX_LEN:]
```

```python
import jax
import jax.numpy as jnp
from jax import lax
from jax.experimental import pallas as pl
from jax.experimental.pallas import tpu as pltpu

D_MODEL = 1024
BATCH = 8
SEQ = 2048
DEPTH = 4
GRID_W = 64
CTX_LEN = 256
T_ALL = CTX_LEN + SEQ
NORM_EPS = 1e-6
N_MOD = 6

POOL_GROUP = 64
POOL_WIDTH = 256
N_HEADS = 8
N_KV_HEADS = 2
HEAD_DIM = 64
Q_PER_KV = N_HEADS // N_KV_HEADS
ATTN_WIDTH = N_HEADS * HEAD_DIM
KV_WIDTH = N_KV_HEADS * HEAD_DIM
ROPE_THETA = 10000.0
ROPE_HALF = HEAD_DIM // 4
LOG2_E = 1.4426950408889634
HG_HEADS = 4
HG_DK = 64
HG_WIDTH = 256
HG_CHUNK = 16
N_CHUNKS = T_ALL // HG_CHUNK
CTX_CHUNKS = CTX_LEN // HG_CHUNK
D_FF = 2816
IN_WIDTH = 5376

C_POOL = 0
C_QKV = 256
C_HG = 1024
C_GATE = 2304

MOD_ROWS = 16
TM = 256
VMEM_LIMIT = 56 * 1024 * 1024

BF16 = jnp.bfloat16
F32 = jnp.float32


def _dot(a, b):
    return jnp.dot(a, b, preferred_element_type=F32)


def _dot_nt(a, b):
    return lax.dot_general(a, b, (((1,), (1,)), ((), ())), preferred_element_type=F32)


def _sigmoid(x):
    return 1.0 / (1.0 + jnp.exp(-x))


def _params(n_grid):
    return pltpu.CompilerParams(dimension_semantics=("arbitrary",) * n_grid,
                                vmem_limit_bytes=VMEM_LIMIT)


def _ada_kernel(c_ref, w_ref, b_ref, o_ref):
    c = c_ref[...]
    act = (c * _sigmoid(c)).astype(BF16)
    o_ref[...] = _dot(act, w_ref[...].astype(BF16)) + b_ref[...]


def _ada_call(cc, w_ada, b_ada):
    tn = 1536
    n = N_MOD * D_MODEL
    return pl.pallas_call(
        _ada_kernel,
        out_shape=jax.ShapeDtypeStruct((DEPTH, MOD_ROWS, n), F32),
        grid=(DEPTH, n // tn),
        in_specs=[
            pl.BlockSpec((MOD_ROWS, D_MODEL), lambda l, j: (0, 0)),
            pl.BlockSpec((None, D_MODEL, tn), lambda l, j: (l, 0, j)),
            pl.BlockSpec((None, 1, tn), lambda l, j: (l, 0, j)),
        ],
        out_specs=pl.BlockSpec((None, MOD_ROWS, tn), lambda l, j: (l, 0, j)),
        compiler_params=_params(2),
        name="ada_mod",
    )(cc, w_ada, b_ada.reshape(DEPTH, 1, n))


def _mod_spec(l, t_off=0):
    return pl.BlockSpec((None, None, N_MOD, D_MODEL),
                        lambda b, t: (l, jnp.where(t + t_off == 0, BATCH, b), 0, 0))


def _const_spec(shape):
    nd = len(shape)
    return pl.BlockSpec(shape, lambda *_: (0,) * nd, pipeline_mode=pl.Buffered(1))


def _layer_spec(shape, l):
    nd = len(shape)
    return pl.BlockSpec((None,) + tuple(shape), lambda *_: (l,) + (0,) * nd,
                        pipeline_mode=pl.Buffered(1))


def _tok_spec(width, t_off=0):
    return pl.BlockSpec((None, TM, width), lambda b, t: (b, t + t_off, 0))


def _rms_mod(x, nw, scale, shift):
    ms = jnp.mean(x * x, axis=-1, keepdims=True)
    y = x * lax.rsqrt(ms + NORM_EPS) * nw
    return y * (1.0 + scale) + shift


def _head_mean_sq(x, ones_bd):
    x2 = x * x
    hi = x2.astype(BF16)
    lo = (x2 - hi.astype(F32)).astype(BF16)
    return (_dot(hi, ones_bd) + _dot(lo, ones_bd)) * (1.0 / HEAD_DIM)


def _swap_rope_halves(x):
    w = x.shape[1]
    lane = lax.broadcasted_iota(jnp.int32, x.shape, 1)
    up = pltpu.roll(x, w - ROPE_HALF, 1)
    dn = pltpu.roll(x, ROPE_HALF, 1)
    return jnp.where((lane & ROPE_HALF) == 0, up, dn)


def _inproj_kernel(x_ref, mod_ref, nw_ref, w_ref, qw_ref, kw_ref, cos_ref, sin_ref, ones_ref,
                   pool_ref, q_ref, k_ref, v_ref, hg_ref, gate_ref):
    h = _rms_mod(x_ref[...], nw_ref[...], mod_ref[1:2, :], mod_ref[0:1, :]).astype(BF16)
    pool_ref[...] = _dot(h, w_ref[:, C_POOL:C_QKV])

    qkv = _dot(h, w_ref[:, C_QKV:C_HG])
    cos = cos_ref[...]
    sin = sin_ref[...]
    q = qkv[:, 0:ATTN_WIDTH]
    q = q * lax.rsqrt(_head_mean_sq(q, ones_ref[...]) + NORM_EPS) * qw_ref[...]
    cos_q = jnp.concatenate([cos] * (ATTN_WIDTH // KV_WIDTH), axis=1)
    sin_q = jnp.concatenate([sin] * (ATTN_WIDTH // KV_WIDTH), axis=1)
    q = (q * cos_q + _swap_rope_halves(q) * sin_q) * (HEAD_DIM ** -0.5 * LOG2_E)
    for hh in range(N_HEADS):
        q_ref[hh] = q[:, hh * HEAD_DIM:(hh + 1) * HEAD_DIM].astype(BF16)
    k = qkv[:, ATTN_WIDTH:ATTN_WIDTH + KV_WIDTH]
    k = k * lax.rsqrt(_head_mean_sq(k, ones_ref[0:KV_WIDTH, 0:KV_WIDTH]) + NORM_EPS) * kw_ref[...]
    k = k * cos + _swap_rope_halves(k) * sin
    v = qkv[:, ATTN_WIDTH + KV_WIDTH:]
    for hh in range(N_KV_HEADS):
        k_ref[hh] = k[:, hh * HEAD_DIM:(hh + 1) * HEAD_DIM].astype(BF16)
        v_ref[hh] = v[:, hh * HEAD_DIM:(hh + 1) * HEAD_DIM].astype(BF16)

    hg_ref[...] = _dot(h, w_ref[:, C_HG:C_GATE])
    gate_ref[...] = _sigmoid(_dot(h, w_ref[:, C_GATE:IN_WIDTH])).astype(BF16)


def _inproj_call(xs, mods, nw, w_in, qw, kw, cos_t, sin_t, ones_bd, l):
    nb = xs.shape[0]
    head_spec = lambda nh: pl.BlockSpec((None, nh, TM, HEAD_DIM), lambda b, t: (b, 0, t, 0))
    rope_spec = pl.BlockSpec((TM, KV_WIDTH), lambda b, t: (t, 0))
    out_shape = [
        jax.ShapeDtypeStruct((nb, T_ALL, POOL_WIDTH), F32),
        jax.ShapeDtypeStruct((nb, N_HEADS, T_ALL, HEAD_DIM), BF16),
        jax.ShapeDtypeStruct((nb, N_KV_HEADS, T_ALL, HEAD_DIM), BF16),
        jax.ShapeDtypeStruct((nb, N_KV_HEADS, T_ALL, HEAD_DIM), BF16),
        jax.ShapeDtypeStruct((nb, T_ALL, C_GATE - C_HG), F32),
        jax.ShapeDtypeStruct((nb, T_ALL, IN_WIDTH - C_GATE), BF16),
    ]
    out_specs = [_tok_spec(POOL_WIDTH), head_spec(N_HEADS), head_spec(N_KV_HEADS),
                 head_spec(N_KV_HEADS), _tok_spec(C_GATE - C_HG), _tok_spec(IN_WIDTH - C_GATE)]
    return pl.pallas_call(
        _inproj_kernel,
        out_shape=out_shape,
        grid=(nb, T_ALL // TM),
        in_specs=[_tok_spec(D_MODEL), _mod_spec(l), _layer_spec((1, D_MODEL), l),
                  _layer_spec((D_MODEL, IN_WIDTH), l), _layer_spec((1, ATTN_WIDTH), l),
                  _layer_spec((1, KV_WIDTH), l), rope_spec, rope_spec,
                  _const_spec((ATTN_WIDTH, ATTN_WIDTH))],
        out_specs=out_specs,
        compiler_params=_params(2),
        name="in_proj",
    )(xs, mods, nw, w_in, qw, kw, cos_t, sin_t, ones_bd)


def _pool_segment(u, w_bd, scale):
    n = u.shape[0]
    row = lax.broadcasted_iota(jnp.int32, u.shape, 0)
    grp = lax.broadcasted_iota(jnp.int32, u.shape, 1) // POOL_GROUP

    def prev(x, s):
        return jnp.where(row >= s, pltpu.roll(x, s, 0), 0.0)

    def nxt(x, s):
        return jnp.where(row < n - s, pltpu.roll(x, n - s, 0), 0.0)

    fwd = [u]
    bwd = [prev(u, 1)]
    for j in range(3):
        s = 1 << j
        fwd.append(fwd[j] + nxt(fwd[j], s))
        bwd.append(bwd[j] + prev(bwd[j], s))
    win = fwd[3] + bwd[3]
    for j in (2, 1, 0):
        win = jnp.where(grp == j, fwd[j] + bwd[j], win)
    half = jnp.left_shift(1, grp)
    cnt = jnp.minimum(row + half, n) - jnp.maximum(row - half, 0)
    y = win / cnt.astype(F32) - u
    return (_dot(y.astype(BF16), w_bd) * scale).astype(BF16)


def _pool_kernel(u_ref, w_ref, s_ref, o_ref):
    o_ref[0:CTX_LEN, :] = _pool_segment(u_ref[0:CTX_LEN, :], w_ref[...], s_ref[...])
    o_ref[CTX_LEN:T_ALL, :] = _pool_segment(u_ref[CTX_LEN:T_ALL, :], w_ref[...], s_ref[...])


def _pool_call(pool_u, w_bd, scale):
    nb = pool_u.shape[0]
    spec = pl.BlockSpec((None, T_ALL, POOL_WIDTH), lambda b: (b, 0, 0))
    return pl.pallas_call(
        _pool_kernel,
        out_shape=jax.ShapeDtypeStruct((nb, T_ALL, POOL_WIDTH), BF16),
        grid=(nb,),
        in_specs=[spec, _const_spec((POOL_WIDTH, POOL_WIDTH)), _const_spec((1, POOL_WIDTH))],
        out_specs=spec,
        compiler_params=_params(1),
        name="pool_mix",
    )(pool_u, w_bd, scale)


ATTN_ROWS = 128


def _attn_kernel(q_ref, k_ref, v_ref, o_ref):
    def run(n_keys):
        blocks = [(j, slice(r * ATTN_ROWS, (r + 1) * ATTN_ROWS))
                  for j in range(N_HEADS) for r in range(TM // ATTN_ROWS)]
        scores = lambda blk: _dot_nt(q_ref[blk[0], blk[1], :], k_ref[blk[0] // Q_PER_KV, 0:n_keys, :])
        s_next = scores(blocks[0])
        for i, (j, rows) in enumerate(blocks):
            s = s_next
            if i + 1 < len(blocks):
                s_next = scores(blocks[i + 1])
            p = jnp.exp2(s - jnp.max(s, axis=-1, keepdims=True))
            denom = jnp.sum(p, axis=-1, keepdims=True)
            o = _dot(p.astype(BF16), v_ref[j // Q_PER_KV, 0:n_keys, :]) / denom
            o_ref[rows, j * HEAD_DIM:(j + 1) * HEAD_DIM] = o.astype(BF16)

    is_ctx = pl.program_id(1) == 0

    @pl.when(is_ctx)
    def _():
        run(CTX_LEN)

    @pl.when(jnp.logical_not(is_ctx))
    def _():
        run(T_ALL)


def _attn_call(q, k, v):
    nb = q.shape[0]
    kv_spec = pl.BlockSpec((None, N_KV_HEADS, T_ALL, HEAD_DIM), lambda b, t: (b, 0, 0, 0))
    return pl.pallas_call(
        _attn_kernel,
        out_shape=jax.ShapeDtypeStruct((nb, T_ALL, ATTN_WIDTH), BF16),
        grid=(nb, T_ALL // TM),
        in_specs=[pl.BlockSpec((None, N_HEADS, TM, HEAD_DIM), lambda b, t: (b, 0, t, 0)),
                  kv_spec, kv_spec],
        out_specs=_tok_spec(ATTN_WIDTH),
        compiler_params=_params(2),
        name="attention",
    )(q, k, v)


HG_PREP_ROWS = 256
HG_UNROLL = 8


def _hgrn_gates(z, log_lb, log_1m_lb):
    log_sig = jnp.minimum(z, 0.0) - jnp.log(1.0 + jnp.exp(-jnp.abs(z)))
    b = log_1m_lb + log_sig
    log_f = jnp.maximum(log_lb, b) + jnp.log(1.0 + jnp.exp(-jnp.abs(log_lb - b)))
    return log_f, b - z


def _hgrn_kernel(hg_ref, lb_ref, nw_ref, ones_ref, o_ref,
                 gf_ref, hf_ref, gb_ref, hb_ref, of_ref, ob_ref, sf_ref, sb_ref):
    C = HG_CHUNK
    W = HG_WIDTH

    def prep(i, carry):
        r0 = pl.multiple_of(i * HG_PREP_ROWS, HG_PREP_ROWS)
        rows = pl.ds(r0, HG_PREP_ROWS)
        pos = lax.broadcasted_iota(jnp.int32, (HG_PREP_ROWS, W), 0) & (C - 1)
        lf, lkf = _hgrn_gates(hg_ref[rows, 2 * W:3 * W], lb_ref[0:1, :], lb_ref[1:2, :])
        lb_, lkb = _hgrn_gates(hg_ref[rows, 3 * W:4 * W], lb_ref[2:3, :], lb_ref[3:4, :])
        for s in (1, 2, 4, 8):
            lf = lf + jnp.where(pos >= s, pltpu.roll(lf, s, 0), 0.0)
            lb_ = lb_ + jnp.where(pos < C - s, pltpu.roll(lb_, HG_PREP_ROWS - s, 0), 0.0)
        gf_ref[rows, :] = lf * LOG2_E
        hf_ref[rows, :] = (lf - lkf) * LOG2_E
        gb_ref[rows, :] = lb_ * LOG2_E
        hb_ref[rows, :] = (lb_ - lkb) * LOG2_E
        return carry

    lax.fori_loop(0, T_ALL // HG_PREP_ROWS, prep, 0)

    sf_ref[...] = jnp.zeros((HG_DK, W), F32)
    sb_ref[...] = jnp.zeros((HG_DK, W), F32)
    ones_bd = ones_ref[...]
    SUB = 8
    tpos = lax.broadcasted_iota(jnp.int32, (SUB, W), 0)
    lane_head = lax.broadcasted_iota(jnp.int32, (C, W), 1) // HG_DK

    def by_head(x):
        return jnp.concatenate([jnp.where(lane_head == hh, x, 0.0) for hh in range(HG_HEADS)], axis=0)

    def chunk_local(n, g_ref, h_ref, forward):
        rows = pl.ds(pl.multiple_of(n * C, C), C)
        g = g_ref[rows, :]
        h = h_ref[rows, :]
        q = hg_ref[rows, 0:W]
        v = hg_ref[rows, W:2 * W]
        g_end = g[C - 1:C, :] if forward else g[0:1, :]
        q_dec = by_head(q * jnp.exp2(g)).astype(BF16)
        parts, where = [], []
        for s in range(C):
            hs = h[s:s + 1, :]
            for t0 in range(0, C, SUB):
                if (forward and t0 + SUB - 1 < s) or (not forward and t0 > s):
                    continue
                decay = jnp.exp2(g[t0:t0 + SUB, :] - hs)
                if t0 <= s < t0 + SUB:
                    causal = (tpos >= s - t0) if forward else (tpos <= s - t0)
                    decay = jnp.where(causal, decay, 0.0)
                parts.append(q[t0:t0 + SUB, :] * decay)
                where.append((s, t0))
        a = _dot(jnp.concatenate(parts, axis=0).astype(BF16), ones_bd)
        o_tiles = {t0: None for t0 in range(0, C, SUB)}
        for i, (s, t0) in enumerate(where):
            term = a[i * SUB:(i + 1) * SUB, :] * v[s:s + 1, :]
            o_tiles[t0] = term if o_tiles[t0] is None else o_tiles[t0] + term
        o_intra = jnp.concatenate([o_tiles[t0] for t0 in range(0, C, SUB)], axis=0)
        k_dec = by_head(jnp.exp2(g_end - h)).astype(BF16)
        v_rows = jnp.concatenate([v[:, hh * HG_DK:(hh + 1) * HG_DK] for hh in range(HG_HEADS)], axis=0)
        kv = lax.dot_general(v_rows.astype(BF16), k_dec, (((0,), (0,)), ((), ())),
                             preferred_element_type=F32)
        return rows, q_dec, o_intra, jnp.exp2(g_end), kv

    def chunk_carry(st, local, out_ref):
        rows, q_dec, o_intra, a_end, kv = local
        r = _dot_nt(q_dec, st.astype(BF16))
        o_inter = jnp.concatenate([r[hh * C:(hh + 1) * C, :] for hh in range(HG_HEADS)], axis=1)
        out_ref[rows, :] = o_inter + o_intra
        return st * a_end + kv

    def step(i, carry):
        local_f, local_b = [], []
        for u in range(HG_UNROLL):
            n = i * HG_UNROLL + u
            local_f.append(chunk_local(n, gf_ref, hf_ref, True))
            nbk = jnp.where(n < CTX_CHUNKS, CTX_CHUNKS - 1 - n, N_CHUNKS - 1 + CTX_CHUNKS - n)
            local_b.append(chunk_local(nbk, gb_ref, hb_ref, False))
        st_f = sf_ref[...]
        st_b = sb_ref[...]
        for u in range(HG_UNROLL):
            st_f = chunk_carry(st_f, local_f[u], of_ref)
            st_b = chunk_carry(st_b, local_b[u], ob_ref)
        sf_ref[...] = st_f
        sb_ref[...] = st_b
        return carry

    lax.fori_loop(0, N_CHUNKS // HG_UNROLL, step, 0)

    def readout(i, carry):
        r0 = pl.multiple_of(i * HG_PREP_ROWS, HG_PREP_ROWS)
        rows = pl.ds(r0, HG_PREP_ROWS)
        o = of_ref[rows, :] + ob_ref[rows, :]
        y = o * lax.rsqrt(_head_mean_sq(o, ones_bd) + NORM_EPS) * nw_ref[...]
        gate = hg_ref[rows, 4 * W:5 * W]
        o_ref[rows, :] = (y * (gate * _sigmoid(gate))).astype(BF16)
        return carry

    lax.fori_loop(0, T_ALL // HG_PREP_ROWS, readout, 0)


def _hgrn_call(hg5, lb_tab, nw, ones_bd):
    nb = hg5.shape[0]
    seq = lambda: pltpu.VMEM((T_ALL, HG_WIDTH), F32)
    return pl.pallas_call(
        _hgrn_kernel,
        out_shape=jax.ShapeDtypeStruct((nb, T_ALL, HG_WIDTH), BF16),
        grid=(nb,),
        in_specs=[pl.BlockSpec((None, T_ALL, 5 * HG_WIDTH), lambda b: (b, 0, 0),
                               pipeline_mode=pl.Buffered(1)),
                  _const_spec((4, HG_WIDTH)), _const_spec((1, HG_WIDTH)),
                  _const_spec((HG_WIDTH, HG_WIDTH))],
        out_specs=pl.BlockSpec((None, T_ALL, HG_WIDTH), lambda b: (b, 0, 0)),
        scratch_shapes=[seq(), seq(), seq(), seq(), seq(), seq(),
                        pltpu.VMEM((HG_DK, HG_WIDTH), F32), pltpu.VMEM((HG_DK, HG_WIDTH), F32)],
        compiler_params=_params(1),
        name="hgrn2",
    )(hg5, lb_tab, nw, ones_bd)


def _mix_ffn_kernel(pool_ref, attn_ref, hgo_ref, gate_ref, x_ref, mod_ref, nw_ref,
                    wbp_ref, wba_ref, wbh_ref, wo_ref, w1_ref, w2_ref, o_ref):
    y = (gate_ref[:, 0:D_MODEL].astype(F32) * _dot(pool_ref[...], wbp_ref[...])
         + gate_ref[:, D_MODEL:2 * D_MODEL].astype(F32) * _dot(attn_ref[...], wba_ref[...])
         + gate_ref[:, 2 * D_MODEL:3 * D_MODEL].astype(F32) * _dot(hgo_ref[...], wbh_ref[...]))
    x = x_ref[...] + mod_ref[2:3, :] * _dot(y.astype(BF16), wo_ref[...])
    h = _rms_mod(x, nw_ref[...], mod_ref[4:5, :], mod_ref[3:4, :]).astype(BF16)
    a = _dot(h, w1_ref[:, 0:D_FF])
    b = _dot(h, w1_ref[:, D_FF:2 * D_FF])
    s = (a * _sigmoid(a) * b).astype(BF16)
    o_ref[...] = x + mod_ref[5:6, :] * _dot(s, w2_ref[...])


def _mix_ffn_call(pool_o, attn_o, hg_o, gates, xs, mods, nw, wbp, wba, wbh, wo, w1, w2, l, latent_only):
    nb = xs.shape[0]
    t_off = 1 if latent_only else 0
    n_rows = SEQ if latent_only else T_ALL
    return pl.pallas_call(
        _mix_ffn_kernel,
        out_shape=jax.ShapeDtypeStruct((nb, n_rows, D_MODEL), F32),
        grid=(nb, n_rows // TM),
        in_specs=[_tok_spec(POOL_WIDTH, t_off), _tok_spec(ATTN_WIDTH, t_off), _tok_spec(HG_WIDTH, t_off),
                  _tok_spec(3 * D_MODEL, t_off), _tok_spec(D_MODEL, t_off), _mod_spec(l, t_off),
                  _layer_spec((1, D_MODEL), l),
                  _layer_spec((POOL_WIDTH, D_MODEL), l), _layer_spec((ATTN_WIDTH, D_MODEL), l),
                  _layer_spec((HG_WIDTH, D_MODEL), l), _layer_spec((D_MODEL, D_MODEL), l),
                  _layer_spec((D_MODEL, 2 * D_FF), l), _layer_spec((D_FF, D_MODEL), l)],
        out_specs=_tok_spec(D_MODEL),
        compiler_params=_params(2),
        name="mix_ffn",
    )(pool_o, attn_o, hg_o, gates, xs, mods, nw, wbp, wba, wbh, wo, w1, w2)


def _rope_tables():
    t = jnp.arange(SEQ, dtype=jnp.int32)
    freqs = ROPE_THETA ** (-jnp.arange(ROPE_HALF, dtype=F32) * (2.0 / (2 * ROPE_HALF)))
    cos_parts, sin_parts = [], []
    for pos in (t // GRID_W, t % GRID_W):
        ang = pos.astype(F32)[:, None] * freqs[None, :]
        cos_parts += [jnp.cos(ang), jnp.cos(ang)]
        sin_parts += [-jnp.sin(ang), jnp.sin(ang)]
    cos = jnp.concatenate(cos_parts, axis=1)
    sin = jnp.concatenate(sin_parts, axis=1)
    cos = jnp.concatenate([jnp.ones((CTX_LEN, HEAD_DIM), F32), cos], axis=0)
    sin = jnp.concatenate([jnp.zeros((CTX_LEN, HEAD_DIM), F32), sin], axis=0)
    return jnp.tile(cos, (1, N_KV_HEADS)), jnp.tile(sin, (1, N_KV_HEADS))


def _block_diag_ones(n):
    i = jnp.arange(n) // HEAD_DIM
    return (i[:, None] == i[None, :]).astype(BF16)


def _pool_block_diag(pool_w_l):
    w = jnp.zeros((POOL_WIDTH, POOL_WIDTH), F32)
    for g in range(POOL_WIDTH // POOL_GROUP):
        sl = slice(g * POOL_GROUP, (g + 1) * POOL_GROUP)
        w = w.at[sl, sl].set(pool_w_l[g])
    return w.astype(BF16)


def kernel(x, c, ctx, c_ctx, w_ada, b_ada, norm1_w, w_in, pool_w, pool_scale, q_norm_w, k_norm_w,
           hg_lb_logits, hg_norm_w, w_branch_pool, w_branch_attn, w_branch_hg, w_out, norm2_w,
           w_ffn_in, w_ffn_out):
    lb_cum = jnp.cumsum(jax.nn.softmax(hg_lb_logits.astype(F32), axis=0), axis=0)
    lb_all = lb_cum - lb_cum[0]
    lb_tab = jnp.stack([jnp.log(lb_all), jnp.log1p(-lb_all)], axis=2)
    lb_tab = lb_tab.reshape(DEPTH, 4, HG_WIDTH)

    cos_t, sin_t = _rope_tables()
    ones_bd = _block_diag_ones(ATTN_WIDTH)
    ones_hg = ones_bd[:HG_WIDTH, :HG_WIDTH]

    bf = lambda w: w.astype(BF16)
    w_in, wbp, wba, wbh, wo = bf(w_in), bf(w_branch_pool), bf(w_branch_attn), bf(w_branch_hg), bf(w_out)
    w1, w2 = bf(w_ffn_in), bf(w_ffn_out)
    nw1 = norm1_w.reshape(DEPTH, 1, D_MODEL)
    nw2 = norm2_w.reshape(DEPTH, 1, D_MODEL)
    qw = jnp.tile(q_norm_w, (1, N_HEADS)).reshape(DEPTH, 1, ATTN_WIDTH)
    kw = jnp.tile(k_norm_w, (1, N_KV_HEADS)).reshape(DEPTH, 1, KV_WIDTH)
    hw = jnp.tile(hg_norm_w, (1, HG_HEADS)).reshape(DEPTH, 1, HG_WIDTH)

    cc = jnp.zeros((MOD_ROWS, D_MODEL), F32).at[:BATCH].set(c).at[BATCH].set(c_ctx)
    mods = _ada_call(cc, w_ada, b_ada).reshape(DEPTH, MOD_ROWS, N_MOD, D_MODEL)

    xs = jnp.concatenate([ctx, x], axis=1)
    for l in range(DEPTH):
        pool_u, q, k, v, hg5, gates = _inproj_call(xs, mods, nw1, w_in, qw, kw, cos_t, sin_t, ones_bd, l)
        pool_o = _pool_call(pool_u, _pool_block_diag(pool_w[l]), pool_scale[l].reshape(1, POOL_WIDTH))
        attn_o = _attn_call(q, k, v)
        hg_o = _hgrn_call(hg5, lb_tab[l], hw[l], ones_hg)
        xs = _mix_ffn_call(pool_o, attn_o, hg_o, gates, xs, mods, nw2, wbp, wba, wbh, wo, w1, w2, l,
                           latent_only=(l == DEPTH - 1))
    return xs
```

```python
import jax
import jax.numpy as jnp
from jax import lax
from jax.experimental import pallas as pl
from jax.experimental.pallas import tpu as pltpu

D_MODEL = 1024
BATCH = 8
SEQ = 2048
DEPTH = 4
GRID_W = 64
CTX_LEN = 256
T_ALL = CTX_LEN + SEQ
NORM_EPS = 1e-6
N_MOD = 6

POOL_GROUP = 64
POOL_WIDTH = 256
N_HEADS = 8
N_KV_HEADS = 2
HEAD_DIM = 64
Q_PER_KV = N_HEADS // N_KV_HEADS
ATTN_WIDTH = N_HEADS * HEAD_DIM
KV_WIDTH = N_KV_HEADS * HEAD_DIM
ROPE_THETA = 10000.0
ROPE_HALF = HEAD_DIM // 4
LOG2_E = 1.4426950408889634
ONES_WIDTH = 256
HG_HEADS = 4
HG_DK = 64
HG_WIDTH = 256
HG_CHUNK = 16
N_CHUNKS = T_ALL // HG_CHUNK
CTX_CHUNKS = CTX_LEN // HG_CHUNK
D_FF = 2816
IN_WIDTH = 5376

C_POOL = 0
C_QKV = 256
C_HG = 1024
C_GATE = 2304

MOD_ROWS = 16
TM = 256
VMEM_LIMIT = 56 * 1024 * 1024

BF16 = jnp.bfloat16
F32 = jnp.float32


def _dot(a, b):
    return jnp.dot(a, b, preferred_element_type=F32)


def _dot_nt(a, b):
    return lax.dot_general(a, b, (((1,), (1,)), ((), ())), preferred_element_type=F32)


def _sigmoid(x):
    return 1.0 / (1.0 + jnp.exp(-x))


def _params(n_grid):
    return pltpu.CompilerParams(dimension_semantics=("arbitrary",) * n_grid,
                                vmem_limit_bytes=VMEM_LIMIT)


def _ada_kernel(c_ref, w_ref, b_ref, o_ref):
    c = c_ref[...]
    act = (c * _sigmoid(c)).astype(BF16)
    o_ref[...] = _dot(act, w_ref[...].astype(BF16)) + b_ref[...]


def _ada_call(cc, w_ada, b_ada):
    tn = 1536
    n = N_MOD * D_MODEL
    return pl.pallas_call(
        _ada_kernel,
        out_shape=jax.ShapeDtypeStruct((DEPTH, MOD_ROWS, n), F32),
        grid=(DEPTH, n // tn),
        in_specs=[
            pl.BlockSpec((MOD_ROWS, D_MODEL), lambda l, j: (0, 0)),
            pl.BlockSpec((None, D_MODEL, tn), lambda l, j: (l, 0, j)),
            pl.BlockSpec((None, 1, tn), lambda l, j: (l, 0, j)),
        ],
        out_specs=pl.BlockSpec((None, MOD_ROWS, tn), lambda l, j: (l, 0, j)),
        compiler_params=_params(2),
        name="ada_mod",
    )(cc, w_ada, b_ada.reshape(DEPTH, 1, n))


def _mod_spec(l, t_off=0):
    return pl.BlockSpec((None, None, N_MOD, D_MODEL),
                        lambda b, t: (l, jnp.where(t + t_off == 0, BATCH, b), 0, 0))


def _const_spec(shape):
    nd = len(shape)
    return pl.BlockSpec(shape, lambda *_: (0,) * nd, pipeline_mode=pl.Buffered(1))


def _layer_spec(shape, l):
    nd = len(shape)
    return pl.BlockSpec((None,) + tuple(shape), lambda *_: (l,) + (0,) * nd,
                        pipeline_mode=pl.Buffered(1))


def _tok_spec(width, t_off=0):
    return pl.BlockSpec((None, TM, width), lambda b, t: (b, t + t_off, 0))


def _rms_mod(x, nw, scale, shift):
    ms = jnp.mean(x * x, axis=-1, keepdims=True)
    y = x * lax.rsqrt(ms + NORM_EPS) * nw
    return y * (1.0 + scale) + shift


def _head_mean_sq(x, ones_bd):
    bw = ones_bd.shape[0]
    if x.shape[1] > bw:
        return jnp.concatenate([_head_mean_sq(x[:, i:i + bw], ones_bd) for i in range(0, x.shape[1], bw)],
                               axis=1)
    x2 = x * x
    hi = x2.astype(BF16)
    lo = (x2 - hi.astype(F32)).astype(BF16)
    return (_dot(hi, ones_bd) + _dot(lo, ones_bd)) * (1.0 / HEAD_DIM)


def _swap_rope_halves(x):
    w = x.shape[1]
    lane = lax.broadcasted_iota(jnp.int32, x.shape, 1)
    up = pltpu.roll(x, w - ROPE_HALF, 1)
    dn = pltpu.roll(x, ROPE_HALF, 1)
    return jnp.where((lane & ROPE_HALF) == 0, up, dn)


def _inproj_kernel(x_ref, mod_ref, nw_ref, w_ref, qw_ref, kw_ref, cos_ref, sin_ref, ones_ref,
                   pool_ref, q_ref, k_ref, v_ref, hg_ref, gate_ref):
    h = _rms_mod(x_ref[...], nw_ref[...], mod_ref[1:2, :], mod_ref[0:1, :]).astype(BF16)
    pool_ref[...] = _dot(h, w_ref[:, C_POOL:C_QKV])

    qkv = _dot(h, w_ref[:, C_QKV:C_HG])
    cos = cos_ref[...]
    sin = sin_ref[...]
    q = qkv[:, 0:ATTN_WIDTH]
    q = q * lax.rsqrt(_head_mean_sq(q, ones_ref[...]) + NORM_EPS) * qw_ref[...]
    cos_q = jnp.concatenate([cos] * (ATTN_WIDTH // KV_WIDTH), axis=1)
    sin_q = jnp.concatenate([sin] * (ATTN_WIDTH // KV_WIDTH), axis=1)
    q = (q * cos_q + _swap_rope_halves(q) * sin_q) * (HEAD_DIM ** -0.5 * LOG2_E)
    for hh in range(N_HEADS):
        q_ref[hh] = q[:, hh * HEAD_DIM:(hh + 1) * HEAD_DIM].astype(BF16)
    k = qkv[:, ATTN_WIDTH:ATTN_WIDTH + KV_WIDTH]
    k = k * lax.rsqrt(_head_mean_sq(k, ones_ref[0:KV_WIDTH, 0:KV_WIDTH]) + NORM_EPS) * kw_ref[...]
    k = k * cos + _swap_rope_halves(k) * sin
    v = qkv[:, ATTN_WIDTH + KV_WIDTH:]
    ones = jnp.ones((TM, HEAD_DIM), F32)
    for hh in range(N_KV_HEADS):
        k_ref[hh] = k[:, hh * HEAD_DIM:(hh + 1) * HEAD_DIM].astype(BF16)
        v_ref[hh] = jnp.concatenate([v[:, hh * HEAD_DIM:(hh + 1) * HEAD_DIM], ones], axis=1).astype(BF16)

    hg_ref[...] = _dot(h, w_ref[:, C_HG:C_GATE])
    gate_ref[...] = _sigmoid(_dot(h, w_ref[:, C_GATE:IN_WIDTH])).astype(BF16)


def _inproj_call(xs, mods, nw, w_in, qw, kw, cos_t, sin_t, ones_bd, l):
    nb = xs.shape[0]
    head_spec = lambda nh, w: pl.BlockSpec((None, nh, TM, w), lambda b, t: (b, 0, t, 0))
    rope_spec = pl.BlockSpec((TM, KV_WIDTH), lambda b, t: (t, 0))
    out_shape = [
        jax.ShapeDtypeStruct((nb, T_ALL, POOL_WIDTH), F32),
        jax.ShapeDtypeStruct((nb, N_HEADS, T_ALL, HEAD_DIM), BF16),
        jax.ShapeDtypeStruct((nb, N_KV_HEADS, T_ALL, HEAD_DIM), BF16),
        jax.ShapeDtypeStruct((nb, N_KV_HEADS, T_ALL, 2 * HEAD_DIM), BF16),
        jax.ShapeDtypeStruct((nb, T_ALL, C_GATE - C_HG), F32),
        jax.ShapeDtypeStruct((nb, T_ALL, IN_WIDTH - C_GATE), BF16),
    ]
    out_specs = [_tok_spec(POOL_WIDTH), head_spec(N_HEADS, HEAD_DIM), head_spec(N_KV_HEADS, HEAD_DIM),
                 head_spec(N_KV_HEADS, 2 * HEAD_DIM), _tok_spec(C_GATE - C_HG), _tok_spec(IN_WIDTH - C_GATE)]
    return pl.pallas_call(
        _inproj_kernel,
        out_shape=out_shape,
        grid=(nb, T_ALL // TM),
        in_specs=[_tok_spec(D_MODEL), _mod_spec(l), _layer_spec((1, D_MODEL), l),
                  _layer_spec((D_MODEL, IN_WIDTH), l), _layer_spec((1, ATTN_WIDTH), l),
                  _layer_spec((1, KV_WIDTH), l), rope_spec, rope_spec,
                  _const_spec((ONES_WIDTH, ONES_WIDTH))],
        out_specs=out_specs,
        compiler_params=_params(2),
        name="in_proj",
    )(xs, mods, nw, w_in, qw, kw, cos_t, sin_t, ones_bd)


def _pool_segment(u, w_bd, scale):
    n = u.shape[0]
    row = lax.broadcasted_iota(jnp.int32, u.shape, 0)
    grp = lax.broadcasted_iota(jnp.int32, u.shape, 1) // POOL_GROUP

    def prev(x, s):
        return jnp.where(row >= s, pltpu.roll(x, s, 0), 0.0)

    def nxt(x, s):
        return jnp.where(row < n - s, pltpu.roll(x, n - s, 0), 0.0)

    fwd = [u]
    bwd = [prev(u, 1)]
    for j in range(3):
        s = 1 << j
        fwd.append(fwd[j] + nxt(fwd[j], s))
        bwd.append(bwd[j] + prev(bwd[j], s))
    win = fwd[3] + bwd[3]
    for j in (2, 1, 0):
        win = jnp.where(grp == j, fwd[j] + bwd[j], win)
    half = jnp.left_shift(1, grp)
    cnt = jnp.minimum(row + half, n) - jnp.maximum(row - half, 0)
    y = win / cnt.astype(F32) - u
    return (_dot(y.astype(BF16), w_bd) * scale).astype(BF16)


def _pool_kernel(u_ref, w_ref, s_ref, o_ref):
    o_ref[0:CTX_LEN, :] = _pool_segment(u_ref[0:CTX_LEN, :], w_ref[...], s_ref[...])
    o_ref[CTX_LEN:T_ALL, :] = _pool_segment(u_ref[CTX_LEN:T_ALL, :], w_ref[...], s_ref[...])


def _pool_call(pool_u, w_bd, scale):
    nb = pool_u.shape[0]
    spec = pl.BlockSpec((None, T_ALL, POOL_WIDTH), lambda b: (b, 0, 0))
    return pl.pallas_call(
        _pool_kernel,
        out_shape=jax.ShapeDtypeStruct((nb, T_ALL, POOL_WIDTH), BF16),
        grid=(nb,),
        in_specs=[spec, _const_spec((POOL_WIDTH, POOL_WIDTH)), _const_spec((1, POOL_WIDTH))],
        out_specs=spec,
        compiler_params=_params(1),
        name="pool_mix",
    )(pool_u, w_bd, scale)


ATTN_ROWS = 128


def _attn_kernel(q_ref, qn_ref, k_ref, v_ref, o_ref, s_ref):
    def run(n_keys, first_carried, make_next):
        blocks = [(j, slice(r * ATTN_ROWS, (r + 1) * ATTN_ROWS))
                  for j in range(N_HEADS) for r in range(TM // ATTN_ROWS)]
        scores = lambda ref, blk, nk: _dot_nt(ref[blk[0], blk[1], :], k_ref[blk[0] // Q_PER_KV, 0:nk, :])
        s_next = s_ref[...] if first_carried else scores(q_ref, blocks[0], n_keys)
        for i, (j, rows) in enumerate(blocks):
            s = s_next
            if i + 1 < len(blocks):
                s_next = scores(q_ref, blocks[i + 1], n_keys)
            elif make_next:
                s_ref[...] = scores(qn_ref, blocks[0], T_ALL)
            m = jnp.max(s, axis=-1, keepdims=True)
            p = jnp.exp2(s - m).astype(BF16)
            acc = _dot(p, v_ref[j // Q_PER_KV, 0:n_keys, :])
            o = acc / pltpu.roll(acc, HEAD_DIM, 1)
            o_ref[rows, j * HEAD_DIM:(j + 1) * HEAD_DIM] = o[:, 0:HEAD_DIM].astype(BF16)

    t = pl.program_id(1)
    last = pl.num_programs(1) - 1

    @pl.when(t == 0)
    def _():
        run(CTX_LEN, False, True)

    @pl.when(jnp.logical_and(t > 0, t < last))
    def _():
        run(T_ALL, True, True)

    @pl.when(t == last)
    def _():
        run(T_ALL, True, False)


def _attn_call(q, k, v):
    nb = q.shape[0]
    n_tiles = T_ALL // TM
    kv_spec = lambda w: pl.BlockSpec((None, N_KV_HEADS, T_ALL, w), lambda b, t: (b, 0, 0, 0))
    q_spec = lambda nxt: pl.BlockSpec((None, N_HEADS, TM, HEAD_DIM),
                                      lambda b, t: (b, 0, jnp.minimum(t + nxt, n_tiles - 1), 0))
    return pl.pallas_call(
        _attn_kernel,
        out_shape=jax.ShapeDtypeStruct((nb, T_ALL, ATTN_WIDTH), BF16),
        grid=(nb, n_tiles),
        in_specs=[q_spec(0), q_spec(1), kv_spec(HEAD_DIM), kv_spec(2 * HEAD_DIM)],
        out_specs=_tok_spec(ATTN_WIDTH),
        scratch_shapes=[pltpu.VMEM((ATTN_ROWS, T_ALL), F32)],
        compiler_params=_params(2),
        name="attention",
    )(q, q, k, v)


HG_PREP_ROWS = 256
HG_UNROLL = 8


def _hgrn_gates(z, log_lb, log_1m_lb):
    z = z * LOG2_E
    log_sig = jnp.minimum(z, 0.0) - jnp.log2(1.0 + jnp.exp2(-jnp.abs(z)))
    b = log_1m_lb + log_sig
    log_f = jnp.maximum(log_lb, b) + jnp.log2(1.0 + jnp.exp2(-jnp.abs(log_lb - b)))
    return log_f, b - z


def _hgrn_kernel(hg_ref, lb_ref, nw_ref, ones_ref, o_ref,
                 gf_ref, hf_ref, gb_ref, hb_ref, of_ref, ob_ref, sf_ref, sb_ref):
    C = HG_CHUNK
    W = HG_WIDTH

    def prep(i, carry):
        r0 = pl.multiple_of(i * HG_PREP_ROWS, HG_PREP_ROWS)
        rows = pl.ds(r0, HG_PREP_ROWS)
        pos = lax.broadcasted_iota(jnp.int32, (HG_PREP_ROWS, W), 0) & (C - 1)
        lf, lkf = _hgrn_gates(hg_ref[rows, 2 * W:3 * W], lb_ref[0:1, :], lb_ref[1:2, :])
        lb_, lkb = _hgrn_gates(hg_ref[rows, 3 * W:4 * W], lb_ref[2:3, :], lb_ref[3:4, :])
        for s in (1, 2, 4, 8):
            lf = lf + jnp.where(pos >= s, pltpu.roll(lf, s, 0), 0.0)
            lb_ = lb_ + jnp.where(pos < C - s, pltpu.roll(lb_, HG_PREP_ROWS - s, 0), 0.0)
        gf_ref[rows, :] = lf
        hf_ref[rows, :] = lf - lkf
        gb_ref[rows, :] = lb_
        hb_ref[rows, :] = lb_ - lkb
        return carry

    lax.fori_loop(0, T_ALL // HG_PREP_ROWS, prep, 0)

    sf_ref[...] = jnp.zeros((HG_DK, W), F32)
    sb_ref[...] = jnp.zeros((HG_DK, W), F32)
    ones_bd = ones_ref[...]
    SUB = 8
    tpos = lax.broadcasted_iota(jnp.int32, (SUB, W), 0)
    lane_head = lax.broadcasted_iota(jnp.int32, (C, W), 1) // HG_DK

    def by_head(x):
        return jnp.concatenate([jnp.where(lane_head == hh, x, 0.0) for hh in range(HG_HEADS)], axis=0)

    def chunk_local(n, g_ref, h_ref, forward):
        rows = pl.ds(pl.multiple_of(n * C, C), C)
        g = g_ref[rows, :]
        h = h_ref[rows, :]
        q = hg_ref[rows, 0:W]
        v = hg_ref[rows, W:2 * W]
        g_end = g[C - 1:C, :] if forward else g[0:1, :]
        q_dec = by_head(q * jnp.exp2(g)).astype(BF16)
        parts, where = [], []
        for s in range(C):
            hs = h[s:s + 1, :]
            for t0 in range(0, C, SUB):
                if (forward and t0 + SUB - 1 < s) or (not forward and t0 > s):
                    continue
                decay = jnp.exp2(g[t0:t0 + SUB, :] - hs)
                if t0 <= s < t0 + SUB:
                    causal = (tpos >= s - t0) if forward else (tpos <= s - t0)
                    decay = jnp.where(causal, decay, 0.0)
                parts.append(q[t0:t0 + SUB, :] * decay)
                where.append((s, t0))
        a = _dot(jnp.concatenate(parts, axis=0).astype(BF16), ones_bd)
        o_tiles = {t0: None for t0 in range(0, C, SUB)}
        for i, (s, t0) in enumerate(where):
            term = a[i * SUB:(i + 1) * SUB, :] * v[s:s + 1, :]
            o_tiles[t0] = term if o_tiles[t0] is None else o_tiles[t0] + term
        o_intra = jnp.concatenate([o_tiles[t0] for t0 in range(0, C, SUB)], axis=0)
        k_dec = by_head(jnp.exp2(g_end - h)).astype(BF16)
        v_rows = jnp.concatenate([v[:, hh * HG_DK:(hh + 1) * HG_DK] for hh in range(HG_HEADS)], axis=0)
        kv = lax.dot_general(v_rows.astype(BF16), k_dec, (((0,), (0,)), ((), ())),
                             preferred_element_type=F32)
        return rows, q_dec, o_intra, jnp.exp2(g_end), kv

    def chunk_carry(st, local, out_ref):
        rows, q_dec, o_intra, a_end, kv = local
        r = _dot_nt(q_dec, st.astype(BF16))
        o_inter = jnp.concatenate([r[hh * C:(hh + 1) * C, :] for hh in range(HG_HEADS)], axis=1)
        out_ref[rows, :] = o_inter + o_intra
        return st * a_end + kv

    def step(i, carry):
        local_f, local_b = [], []
        for u in range(HG_UNROLL):
            n = i * HG_UNROLL + u
            local_f.append(chunk_local(n, gf_ref, hf_ref, True))
            nbk = jnp.where(n < CTX_CHUNKS, CTX_CHUNKS - 1 - n, N_CHUNKS - 1 + CTX_CHUNKS - n)
            local_b.append(chunk_local(nbk, gb_ref, hb_ref, False))
        st_f = sf_ref[...]
        st_b = sb_ref[...]
        for u in range(HG_UNROLL):
            st_f = chunk_carry(st_f, local_f[u], of_ref)
            st_b = chunk_carry(st_b, local_b[u], ob_ref)
        sf_ref[...] = st_f
        sb_ref[...] = st_b
        return carry

    lax.fori_loop(0, N_CHUNKS // HG_UNROLL, step, 0)

    def readout(i, carry):
        r0 = pl.multiple_of(i * HG_PREP_ROWS, HG_PREP_ROWS)
        rows = pl.ds(r0, HG_PREP_ROWS)
        o = of_ref[rows, :] + ob_ref[rows, :]
        y = o * lax.rsqrt(_head_mean_sq(o, ones_bd) + NORM_EPS) * nw_ref[...]
        gate = hg_ref[rows, 4 * W:5 * W]
        o_ref[rows, :] = (y * (gate * _sigmoid(gate))).astype(BF16)
        return carry

    lax.fori_loop(0, T_ALL // HG_PREP_ROWS, readout, 0)


def _hgrn_call(hg5, lb_tab, nw, ones_bd):
    nb = hg5.shape[0]
    seq = lambda: pltpu.VMEM((T_ALL, HG_WIDTH), F32)
    return pl.pallas_call(
        _hgrn_kernel,
        out_shape=jax.ShapeDtypeStruct((nb, T_ALL, HG_WIDTH), BF16),
        grid=(nb,),
        in_specs=[pl.BlockSpec((None, T_ALL, 5 * HG_WIDTH), lambda b: (b, 0, 0)),
                  _const_spec((4, HG_WIDTH)), _const_spec((1, HG_WIDTH)),
                  _const_spec((HG_WIDTH, HG_WIDTH))],
        out_specs=pl.BlockSpec((None, T_ALL, HG_WIDTH), lambda b: (b, 0, 0)),
        scratch_shapes=[seq(), seq(), seq(), seq(), seq(), seq(),
                        pltpu.VMEM((HG_DK, HG_WIDTH), F32), pltpu.VMEM((HG_DK, HG_WIDTH), F32)],
        compiler_params=_params(1),
        name="hgrn2",
    )(hg5, lb_tab, nw, ones_bd)


def _mix_ffn_kernel(pool_ref, attn_ref, hgo_ref, gate_ref, x_ref, mod_ref, nw_ref,
                    wbp_ref, wba_ref, wbh_ref, wo_ref, w1_ref, w2_ref, o_ref):
    y = (gate_ref[:, 0:D_MODEL].astype(F32) * _dot(pool_ref[...], wbp_ref[...])
         + gate_ref[:, D_MODEL:2 * D_MODEL].astype(F32) * _dot(attn_ref[...], wba_ref[...])
         + gate_ref[:, 2 * D_MODEL:3 * D_MODEL].astype(F32) * _dot(hgo_ref[...], wbh_ref[...]))
    x = x_ref[...] + mod_ref[2:3, :] * _dot(y.astype(BF16), wo_ref[...])
    h = _rms_mod(x, nw_ref[...], mod_ref[4:5, :], mod_ref[3:4, :]).astype(BF16)
    a = _dot(h, w1_ref[:, 0:D_FF])
    b = _dot(h, w1_ref[:, D_FF:2 * D_FF])
    s = (a * _sigmoid(a) * b).astype(BF16)
    o_ref[...] = x + mod_ref[5:6, :] * _dot(s, w2_ref[...])


def _mix_ffn_call(pool_o, attn_o, hg_o, gates, xs, mods, nw, wbp, wba, wbh, wo, w1, w2, l, latent_only):
    nb = xs.shape[0]
    t_off = 1 if latent_only else 0
    n_rows = SEQ if latent_only else T_ALL
    return pl.pallas_call(
        _mix_ffn_kernel,
        out_shape=jax.ShapeDtypeStruct((nb, n_rows, D_MODEL), F32),
        grid=(nb, n_rows // TM),
        in_specs=[_tok_spec(POOL_WIDTH, t_off), _tok_spec(ATTN_WIDTH, t_off), _tok_spec(HG_WIDTH, t_off),
                  _tok_spec(3 * D_MODEL, t_off), _tok_spec(D_MODEL, t_off), _mod_spec(l, t_off),
                  _layer_spec((1, D_MODEL), l),
                  _layer_spec((POOL_WIDTH, D_MODEL), l), _layer_spec((ATTN_WIDTH, D_MODEL), l),
                  _layer_spec((HG_WIDTH, D_MODEL), l), _layer_spec((D_MODEL, D_MODEL), l),
                  _layer_spec((D_MODEL, 2 * D_FF), l), _layer_spec((D_FF, D_MODEL), l)],
        out_specs=_tok_spec(D_MODEL),
        compiler_params=_params(2),
        name="mix_ffn",
    )(pool_o, attn_o, hg_o, gates, xs, mods, nw, wbp, wba, wbh, wo, w1, w2)


def _rope_tables():
    t = jnp.arange(SEQ, dtype=jnp.int32)
    freqs = ROPE_THETA ** (-jnp.arange(ROPE_HALF, dtype=F32) * (2.0 / (2 * ROPE_HALF)))
    cos_parts, sin_parts = [], []
    for pos in (t // GRID_W, t % GRID_W):
        ang = pos.astype(F32)[:, None] * freqs[None, :]
        cos_parts += [jnp.cos(ang), jnp.cos(ang)]
        sin_parts += [-jnp.sin(ang), jnp.sin(ang)]
    cos = jnp.concatenate(cos_parts, axis=1)
    sin = jnp.concatenate(sin_parts, axis=1)
    cos = jnp.concatenate([jnp.ones((CTX_LEN, HEAD_DIM), F32), cos], axis=0)
    sin = jnp.concatenate([jnp.zeros((CTX_LEN, HEAD_DIM), F32), sin], axis=0)
    return jnp.tile(cos, (1, N_KV_HEADS)), jnp.tile(sin, (1, N_KV_HEADS))


def _block_diag_ones(n):
    i = jnp.arange(n) // HEAD_DIM
    return (i[:, None] == i[None, :]).astype(BF16)


def _pool_block_diag(pool_w_l):
    w = jnp.zeros((POOL_WIDTH, POOL_WIDTH), F32)
    for g in range(POOL_WIDTH // POOL_GROUP):
        sl = slice(g * POOL_GROUP, (g + 1) * POOL_GROUP)
        w = w.at[sl, sl].set(pool_w_l[g])
    return w.astype(BF16)


def kernel(x, c, ctx, c_ctx, w_ada, b_ada, norm1_w, w_in, pool_w, pool_scale, q_norm_w, k_norm_w,
           hg_lb_logits, hg_norm_w, w_branch_pool, w_branch_attn, w_branch_hg, w_out, norm2_w,
           w_ffn_in, w_ffn_out):
    lb_cum = jnp.cumsum(jax.nn.softmax(hg_lb_logits.astype(F32), axis=0), axis=0)
    lb_all = lb_cum - lb_cum[0]
    lb_tab = jnp.stack([jnp.log(lb_all), jnp.log1p(-lb_all)], axis=2) * LOG2_E
    lb_tab = lb_tab.reshape(DEPTH, 4, HG_WIDTH)

    cos_t, sin_t = _rope_tables()
    ones_bd = _block_diag_ones(ONES_WIDTH)

    bf = lambda w: w.astype(BF16)
    w_in, wbp, wba, wbh, wo = bf(w_in), bf(w_branch_pool), bf(w_branch_attn), bf(w_branch_hg), bf(w_out)
    w1, w2 = bf(w_ffn_in), bf(w_ffn_out)
    nw1 = norm1_w.reshape(DEPTH, 1, D_MODEL)
    nw2 = norm2_w.reshape(DEPTH, 1, D_MODEL)
    qw = jnp.tile(q_norm_w, (1, N_HEADS)).reshape(DEPTH, 1, ATTN_WIDTH)
    kw = jnp.tile(k_norm_w, (1, N_KV_HEADS)).reshape(DEPTH, 1, KV_WIDTH)
    hw = jnp.tile(hg_norm_w, (1, HG_HEADS)).reshape(DEPTH, 1, HG_WIDTH)

    cc = jnp.zeros((MOD_ROWS, D_MODEL), F32).at[:BATCH].set(c).at[BATCH].set(c_ctx)
    mods = _ada_call(cc, w_ada, b_ada).reshape(DEPTH, MOD_ROWS, N_MOD, D_MODEL)

    xs = jnp.concatenate([ctx, x], axis=1)
    for l in range(DEPTH):
        pool_u, q, k, v, hg5, gates = _inproj_call(xs, mods, nw1, w_in, qw, kw, cos_t, sin_t, ones_bd, l)
        pool_o = _pool_call(pool_u, _pool_block_diag(pool_w[l]), pool_scale[l].reshape(1, POOL_WIDTH))
        attn_o = _attn_call(q, k, v)
        hg_o = _hgrn_call(hg5, lb_tab[l], hw[l], ones_bd)
        xs = _mix_ffn_call(pool_o, attn_o, hg_o, gates, xs, mods, nw2, wbp, wba, wbh, wo, w1, w2, l,
                           latent_only=(l == DEPTH - 1))
    return xs
```

```python
import jax
import jax.numpy as jnp
from jax import lax
from jax.experimental import pallas as pl
from jax.experimental.pallas import tpu as pltpu

D_MODEL = 1024
BATCH = 8
SEQ = 2048
DEPTH = 4
GRID_W = 64
CTX_LEN = 256
T_ALL = CTX_LEN + SEQ
NORM_EPS = 1e-6
N_MOD = 6

POOL_GROUP = 64
POOL_WIDTH = 256
N_HEADS = 8
N_KV_HEADS = 2
HEAD_DIM = 64
Q_PER_KV = N_HEADS // N_KV_HEADS
ATTN_WIDTH = N_HEADS * HEAD_DIM
KV_WIDTH = N_KV_HEADS * HEAD_DIM
ROPE_THETA = 10000.0
ROPE_HALF = HEAD_DIM // 4
LOG2_E = 1.4426950408889634
ONES_WIDTH = 256
HG_HEADS = 4
HG_DK = 64
HG_WIDTH = 256
HG_CHUNK = 16
N_CHUNKS = T_ALL // HG_CHUNK
CTX_CHUNKS = CTX_LEN // HG_CHUNK
D_FF = 2816
IN_WIDTH = 5376

C_POOL = 0
C_QKV = 256
C_HG = 1024
C_GATE = 2304

MOD_ROWS = 16
TM = 256
VMEM_LIMIT = 56 * 1024 * 1024

BF16 = jnp.bfloat16
F32 = jnp.float32


def _dot(a, b):
    return jnp.dot(a, b, preferred_element_type=F32)


def _dot_nt(a, b):
    return lax.dot_general(a, b, (((1,), (1,)), ((), ())), preferred_element_type=F32)


def _sigmoid(x):
    return 1.0 / (1.0 + jnp.exp(-x))


def _params(n_grid):
    return pltpu.CompilerParams(dimension_semantics=("arbitrary",) * n_grid,
                                vmem_limit_bytes=VMEM_LIMIT)


def _ada_kernel(c_ref, w_ref, b_ref, o_ref):
    c = c_ref[...]
    act = (c * _sigmoid(c)).astype(BF16)
    o_ref[...] = _dot(act, w_ref[...].astype(BF16)) + b_ref[...]


def _ada_call(cc, w_ada, b_ada):
    tn = 1536
    n = N_MOD * D_MODEL
    return pl.pallas_call(
        _ada_kernel,
        out_shape=jax.ShapeDtypeStruct((DEPTH, MOD_ROWS, n), F32),
        grid=(DEPTH, n // tn),
        in_specs=[
            pl.BlockSpec((MOD_ROWS, D_MODEL), lambda l, j: (0, 0)),
            pl.BlockSpec((None, D_MODEL, tn), lambda l, j: (l, 0, j)),
            pl.BlockSpec((None, 1, tn), lambda l, j: (l, 0, j)),
        ],
        out_specs=pl.BlockSpec((None, MOD_ROWS, tn), lambda l, j: (l, 0, j)),
        compiler_params=_params(2),
        name="ada_mod",
    )(cc, w_ada, b_ada.reshape(DEPTH, 1, n))


def _mod_spec(l, t_off=0):
    return pl.BlockSpec((None, None, N_MOD, D_MODEL),
                        lambda b, t: (l, jnp.where(t + t_off == 0, BATCH, b), 0, 0))


def _const_spec(shape):
    nd = len(shape)
    return pl.BlockSpec(shape, lambda *_: (0,) * nd, pipeline_mode=pl.Buffered(1))


def _layer_spec(shape, l):
    nd = len(shape)
    return pl.BlockSpec((None,) + tuple(shape), lambda *_: (l,) + (0,) * nd,
                        pipeline_mode=pl.Buffered(1))


def _tok_spec(width, t_off=0):
    return pl.BlockSpec((None, TM, width), lambda b, t: (b, t + t_off, 0))


def _rms_mod(x, nw, scale, shift):
    ms = jnp.mean(x * x, axis=-1, keepdims=True)
    y = x * lax.rsqrt(ms + NORM_EPS) * nw
    return y * (1.0 + scale) + shift


def _head_mean_sq(x, ones_bd):
    bw = ones_bd.shape[0]
    if x.shape[1] > bw:
        return jnp.concatenate([_head_mean_sq(x[:, i:i + bw], ones_bd) for i in range(0, x.shape[1], bw)],
                               axis=1)
    x2 = x * x
    hi = x2.astype(BF16)
    lo = (x2 - hi.astype(F32)).astype(BF16)
    return (_dot(hi, ones_bd) + _dot(lo, ones_bd)) * (1.0 / HEAD_DIM)


def _swap_rope_halves(x):
    w = x.shape[1]
    lane = lax.broadcasted_iota(jnp.int32, x.shape, 1)
    up = pltpu.roll(x, w - ROPE_HALF, 1)
    dn = pltpu.roll(x, ROPE_HALF, 1)
    return jnp.where((lane & ROPE_HALF) == 0, up, dn)


def _inproj_kernel(x_ref, mod_ref, nw_ref, w_ref, qw_ref, kw_ref, cos_ref, sin_ref, ones_ref,
                   pool_ref, q_ref, k_ref, v_ref, hg_ref, gate_ref):
    h = _rms_mod(x_ref[...], nw_ref[...], mod_ref[1:2, :], mod_ref[0:1, :]).astype(BF16)
    pool_ref[...] = _dot(h, w_ref[:, C_POOL:C_QKV])

    qkv = _dot(h, w_ref[:, C_QKV:C_HG])
    cos = cos_ref[...]
    sin = sin_ref[...]
    q = qkv[:, 0:ATTN_WIDTH]
    q = q * lax.rsqrt(_head_mean_sq(q, ones_ref[...]) + NORM_EPS) * qw_ref[...]
    cos_q = jnp.concatenate([cos] * (ATTN_WIDTH // KV_WIDTH), axis=1)
    sin_q = jnp.concatenate([sin] * (ATTN_WIDTH // KV_WIDTH), axis=1)
    q = (q * cos_q + _swap_rope_halves(q) * sin_q) * (HEAD_DIM ** -0.5 * LOG2_E)
    for hh in range(N_HEADS):
        q_ref[hh] = q[:, hh * HEAD_DIM:(hh + 1) * HEAD_DIM].astype(BF16)
    k = qkv[:, ATTN_WIDTH:ATTN_WIDTH + KV_WIDTH]
    k = k * lax.rsqrt(_head_mean_sq(k, ones_ref[0:KV_WIDTH, 0:KV_WIDTH]) + NORM_EPS) * kw_ref[...]
    k = k * cos + _swap_rope_halves(k) * sin
    v = qkv[:, ATTN_WIDTH + KV_WIDTH:]
    ones = jnp.ones((TM, HEAD_DIM), F32)
    for hh in range(N_KV_HEADS):
        k_ref[hh] = k[:, hh * HEAD_DIM:(hh + 1) * HEAD_DIM].astype(BF16)
        v_ref[hh] = jnp.concatenate([v[:, hh * HEAD_DIM:(hh + 1) * HEAD_DIM], ones], axis=1).astype(BF16)

    hg_ref[...] = _dot(h, w_ref[:, C_HG:C_GATE])
    gate_ref[...] = _sigmoid(_dot(h, w_ref[:, C_GATE:IN_WIDTH])).astype(BF16)


def _inproj_call(xs, mods, nw, w_in, qw, kw, cos_t, sin_t, ones_bd, l):
    nb = xs.shape[0]
    head_spec = lambda nh, w: pl.BlockSpec((None, nh, TM, w), lambda b, t: (b, 0, t, 0))
    rope_spec = pl.BlockSpec((TM, KV_WIDTH), lambda b, t: (t, 0))
    out_shape = [
        jax.ShapeDtypeStruct((nb, T_ALL, POOL_WIDTH), F32),
        jax.ShapeDtypeStruct((nb, N_HEADS, T_ALL, HEAD_DIM), BF16),
        jax.ShapeDtypeStruct((nb, N_KV_HEADS, T_ALL, HEAD_DIM), BF16),
        jax.ShapeDtypeStruct((nb, N_KV_HEADS, T_ALL, 2 * HEAD_DIM), BF16),
        jax.ShapeDtypeStruct((nb, T_ALL, C_GATE - C_HG), F32),
        jax.ShapeDtypeStruct((nb, T_ALL, IN_WIDTH - C_GATE), BF16),
    ]
    out_specs = [_tok_spec(POOL_WIDTH), head_spec(N_HEADS, HEAD_DIM), head_spec(N_KV_HEADS, HEAD_DIM),
                 head_spec(N_KV_HEADS, 2 * HEAD_DIM), _tok_spec(C_GATE - C_HG), _tok_spec(IN_WIDTH - C_GATE)]
    return pl.pallas_call(
        _inproj_kernel,
        out_shape=out_shape,
        grid=(nb, T_ALL // TM),
        in_specs=[_tok_spec(D_MODEL), _mod_spec(l), _layer_spec((1, D_MODEL), l),
                  _layer_spec((D_MODEL, IN_WIDTH), l), _layer_spec((1, ATTN_WIDTH), l),
                  _layer_spec((1, KV_WIDTH), l), rope_spec, rope_spec,
                  _const_spec((ONES_WIDTH, ONES_WIDTH))],
        out_specs=out_specs,
        compiler_params=_params(2),
        name="in_proj",
    )(xs, mods, nw, w_in, qw, kw, cos_t, sin_t, ones_bd)


def _pool_segment(u, w_bd, scale):
    n = u.shape[0]
    row = lax.broadcasted_iota(jnp.int32, u.shape, 0)
    grp = lax.broadcasted_iota(jnp.int32, u.shape, 1) // POOL_GROUP

    def prev(x, s):
        return jnp.where(row >= s, pltpu.roll(x, s, 0), 0.0)

    def nxt(x, s):
        return jnp.where(row < n - s, pltpu.roll(x, n - s, 0), 0.0)

    fwd = [u]
    bwd = [prev(u, 1)]
    for j in range(3):
        s = 1 << j
        fwd.append(fwd[j] + nxt(fwd[j], s))
        bwd.append(bwd[j] + prev(bwd[j], s))
    win = fwd[3] + bwd[3]
    for j in (2, 1, 0):
        win = jnp.where(grp == j, fwd[j] + bwd[j], win)
    half = jnp.left_shift(1, grp)
    cnt = jnp.minimum(row + half, n) - jnp.maximum(row - half, 0)
    y = win / cnt.astype(F32) - u
    return (_dot(y.astype(BF16), w_bd) * scale).astype(BF16)


def _pool_kernel(u_ref, w_ref, s_ref, o_ref):
    o_ref[0:CTX_LEN, :] = _pool_segment(u_ref[0:CTX_LEN, :], w_ref[...], s_ref[...])
    o_ref[CTX_LEN:T_ALL, :] = _pool_segment(u_ref[CTX_LEN:T_ALL, :], w_ref[...], s_ref[...])


def _pool_call(pool_u, w_bd, scale):
    nb = pool_u.shape[0]
    spec = pl.BlockSpec((None, T_ALL, POOL_WIDTH), lambda b: (b, 0, 0))
    return pl.pallas_call(
        _pool_kernel,
        out_shape=jax.ShapeDtypeStruct((nb, T_ALL, POOL_WIDTH), BF16),
        grid=(nb,),
        in_specs=[spec, _const_spec((POOL_WIDTH, POOL_WIDTH)), _const_spec((1, POOL_WIDTH))],
        out_specs=spec,
        compiler_params=_params(1),
        name="pool_mix",
    )(pool_u, w_bd, scale)


ATTN_ROWS = 128
ATTN_KEYS = 256


def _attn_kernel(q_ref, qn_ref, k_ref, v_ref, o_ref, s_ref):
    def run(n_keys, first_carried, make_next):
        blocks = [(j, slice(r * ATTN_ROWS, (r + 1) * ATTN_ROWS))
                  for j in range(N_HEADS) for r in range(TM // ATTN_ROWS)]
        scores = lambda ref, blk, nk: _dot_nt(ref[blk[0], blk[1], :], k_ref[blk[0] // Q_PER_KV, 0:nk, :])
        s_next = s_ref[...] if first_carried else scores(q_ref, blocks[0], n_keys)
        for i, (j, rows) in enumerate(blocks):
            s = s_next
            if i + 1 < len(blocks):
                s_next = scores(q_ref, blocks[i + 1], n_keys)
            elif make_next:
                s_ref[...] = scores(qn_ref, blocks[0], T_ALL)
            m = jnp.max(s, axis=-1, keepdims=True)
            p = jnp.exp2(s - m).astype(BF16)
            acc = _dot(p, v_ref[j // Q_PER_KV, 0:n_keys, :])
            o = acc / pltpu.roll(acc, HEAD_DIM, 1)
            o_ref[rows, j * HEAD_DIM:(j + 1) * HEAD_DIM] = o[:, 0:HEAD_DIM].astype(BF16)

    t = pl.program_id(1)
    last = pl.num_programs(1) - 1

    @pl.when(t == 0)
    def _():
        run(CTX_LEN, False, True)

    @pl.when(jnp.logical_and(t > 0, t < last))
    def _():
        run(T_ALL, True, True)

    @pl.when(t == last)
    def _():
        run(T_ALL, True, False)


def _attn_call(q, k, v):
    nb = q.shape[0]
    n_tiles = T_ALL // TM
    kv_spec = lambda w: pl.BlockSpec((None, N_KV_HEADS, T_ALL, w), lambda b, t: (b, 0, 0, 0))
    q_spec = lambda nxt: pl.BlockSpec((None, N_HEADS, TM, HEAD_DIM),
                                      lambda b, t: (b, 0, jnp.minimum(t + nxt, n_tiles - 1), 0))
    return pl.pallas_call(
        _attn_kernel,
        out_shape=jax.ShapeDtypeStruct((nb, T_ALL, ATTN_WIDTH), BF16),
        grid=(nb, n_tiles),
        in_specs=[q_spec(0), q_spec(1), kv_spec(HEAD_DIM), kv_spec(2 * HEAD_DIM)],
        out_specs=_tok_spec(ATTN_WIDTH),
        scratch_shapes=[pltpu.VMEM((ATTN_ROWS, T_ALL), F32)],
        compiler_params=_params(2),
        name="attention",
    )(q, q, k, v)


HG_PREP_ROWS = 256
HG_UNROLL = 8


def _hgrn_gates(z, log_lb, log_1m_lb):
    z = z * LOG2_E
    log_sig = jnp.minimum(z, 0.0) - jnp.log2(1.0 + jnp.exp2(-jnp.abs(z)))
    b = log_1m_lb + log_sig
    log_f = jnp.maximum(log_lb, b) + jnp.log2(1.0 + jnp.exp2(-jnp.abs(log_lb - b)))
    return log_f, b - z


def _hgrn_kernel(hg_ref, lb_ref, nw_ref, ones_ref, o_ref,
                 gf_ref, hf_ref, gb_ref, hb_ref, of_ref, ob_ref, sf_ref, sb_ref):
    C = HG_CHUNK
    W = HG_WIDTH

    def prep(i, carry):
        r0 = pl.multiple_of(i * HG_PREP_ROWS, HG_PREP_ROWS)
        rows = pl.ds(r0, HG_PREP_ROWS)
        pos = lax.broadcasted_iota(jnp.int32, (HG_PREP_ROWS, W), 0) & (C - 1)
        lf, lkf = _hgrn_gates(hg_ref[rows, 2 * W:3 * W], lb_ref[0:1, :], lb_ref[1:2, :])
        lb_, lkb = _hgrn_gates(hg_ref[rows, 3 * W:4 * W], lb_ref[2:3, :], lb_ref[3:4, :])
        for s in (1, 2, 4, 8):
            lf = lf + jnp.where(pos >= s, pltpu.roll(lf, s, 0), 0.0)
            lb_ = lb_ + jnp.where(pos < C - s, pltpu.roll(lb_, HG_PREP_ROWS - s, 0), 0.0)
        gf_ref[rows, :] = lf
        hf_ref[rows, :] = lf - lkf
        gb_ref[rows, :] = lb_
        hb_ref[rows, :] = lb_ - lkb
        return carry

    lax.fori_loop(0, T_ALL // HG_PREP_ROWS, prep, 0)

    sf_ref[...] = jnp.zeros((HG_DK, W), F32)
    sb_ref[...] = jnp.zeros((HG_DK, W), F32)
    ones_bd = ones_ref[...]
    SUB = 8
    tpos = lax.broadcasted_iota(jnp.int32, (SUB, W), 0)
    lane_head = lax.broadcasted_iota(jnp.int32, (C, W), 1) // HG_DK

    def by_head(x):
        return jnp.concatenate([jnp.where(lane_head == hh, x, 0.0) for hh in range(HG_HEADS)], axis=0)

    def chunk_local(n, g_ref, h_ref, forward):
        rows = pl.ds(pl.multiple_of(n * C, C), C)
        g = g_ref[rows, :]
        h = h_ref[rows, :]
        q = hg_ref[rows, 0:W]
        v = hg_ref[rows, W:2 * W]
        g_end = g[C - 1:C, :] if forward else g[0:1, :]
        q_dec = by_head(q * jnp.exp2(g)).astype(BF16)
        parts, where = [], []
        for s in range(C):
            hs = h[s:s + 1, :]
            for t0 in range(0, C, SUB):
                if (forward and t0 + SUB - 1 < s) or (not forward and t0 > s):
                    continue
                decay = jnp.exp2(g[t0:t0 + SUB, :] - hs)
                if t0 <= s < t0 + SUB:
                    causal = (tpos >= s - t0) if forward else (tpos <= s - t0)
                    decay = jnp.where(causal, decay, 0.0)
                parts.append(q[t0:t0 + SUB, :] * decay)
                where.append((s, t0))
        a = _dot(jnp.concatenate(parts, axis=0).astype(BF16), ones_bd)
        o_tiles = {t0: None for t0 in range(0, C, SUB)}
        for i, (s, t0) in enumerate(where):
            term = a[i * SUB:(i + 1) * SUB, :] * v[s:s + 1, :]
            o_tiles[t0] = term if o_tiles[t0] is None else o_tiles[t0] + term
        o_intra = jnp.concatenate([o_tiles[t0] for t0 in range(0, C, SUB)], axis=0)
        k_dec = by_head(jnp.exp2(g_end - h)).astype(BF16)
        v_rows = jnp.concatenate([v[:, hh * HG_DK:(hh + 1) * HG_DK] for hh in range(HG_HEADS)], axis=0)
        kv = lax.dot_general(v_rows.astype(BF16), k_dec, (((0,), (0,)), ((), ())),
                             preferred_element_type=F32)
        return rows, q_dec, o_intra, jnp.exp2(g_end), kv

    def chunk_carry(st, local, out_ref):
        rows, q_dec, o_intra, a_end, kv = local
        r = _dot_nt(q_dec, st.astype(BF16))
        o_inter = jnp.concatenate([r[hh * C:(hh + 1) * C, :] for hh in range(HG_HEADS)], axis=1)
        out_ref[rows, :] = o_inter + o_intra
        return st * a_end + kv

    def step(i, carry):
        local_f, local_b = [], []
        for u in range(HG_UNROLL):
            n = i * HG_UNROLL + u
            local_f.append(chunk_local(n, gf_ref, hf_ref, True))
            nbk = jnp.where(n < CTX_CHUNKS, CTX_CHUNKS - 1 - n, N_CHUNKS - 1 + CTX_CHUNKS - n)
            local_b.append(chunk_local(nbk, gb_ref, hb_ref, False))
        st_f = sf_ref[...]
        st_b = sb_ref[...]
        for u in range(HG_UNROLL):
            st_f = chunk_carry(st_f, local_f[u], of_ref)
            st_b = chunk_carry(st_b, local_b[u], ob_ref)
        sf_ref[...] = st_f
        sb_ref[...] = st_b
        return carry

    lax.fori_loop(0, N_CHUNKS // HG_UNROLL, step, 0)

    def readout(i, carry):
        r0 = pl.multiple_of(i * HG_PREP_ROWS, HG_PREP_ROWS)
        rows = pl.ds(r0, HG_PREP_ROWS)
        o = of_ref[rows, :] + ob_ref[rows, :]
        y = o * lax.rsqrt(_head_mean_sq(o, ones_bd) + NORM_EPS) * nw_ref[...]
        gate = hg_ref[rows, 4 * W:5 * W]
        o_ref[rows, :] = (y * (gate * _sigmoid(gate))).astype(BF16)
        return carry

    lax.fori_loop(0, T_ALL // HG_PREP_ROWS, readout, 0)


def _hgrn_call(hg5, lb_tab, nw, ones_bd):
    nb = hg5.shape[0]
    seq = lambda: pltpu.VMEM((T_ALL, HG_WIDTH), F32)
    return pl.pallas_call(
        _hgrn_kernel,
        out_shape=jax.ShapeDtypeStruct((nb, T_ALL, HG_WIDTH), BF16),
        grid=(nb,),
        in_specs=[pl.BlockSpec((None, T_ALL, 5 * HG_WIDTH), lambda b: (b, 0, 0)),
                  _const_spec((4, HG_WIDTH)), _const_spec((1, HG_WIDTH)),
                  _const_spec((HG_WIDTH, HG_WIDTH))],
        out_specs=pl.BlockSpec((None, T_ALL, HG_WIDTH), lambda b: (b, 0, 0)),
        scratch_shapes=[seq(), seq(), seq(), seq(), seq(), seq(),
                        pltpu.VMEM((HG_DK, HG_WIDTH), F32), pltpu.VMEM((HG_DK, HG_WIDTH), F32)],
        compiler_params=_params(1),
        name="hgrn2",
    )(hg5, lb_tab, nw, ones_bd)


HG_TILE_CHUNKS = TM // HG_CHUNK


def _mixer_kernel(q_ref, qn_ref, k_ref, v_ref, hg_ref, lb_ref, nw_ref, ones_ref, attn_ref, hgo_ref,
                  s_ref, gf_ref, hf_ref, gb_ref, hb_ref, of_ref, ob_ref, sf_ref, sb_ref):
    C = HG_CHUNK
    W = HG_WIDTH
    SUB = 8
    t = pl.program_id(1)
    n_tiles = pl.num_programs(1)
    ones_bd = ones_ref[...]
    tpos = lax.broadcasted_iota(jnp.int32, (SUB, W), 0)
    lane_head = lax.broadcasted_iota(jnp.int32, (C, W), 1) // HG_DK

    @pl.when(t == 0)
    def _():
        sf_ref[...] = jnp.zeros((HG_DK, W), F32)
        sb_ref[...] = jnp.zeros((HG_DK, W), F32)

    tile_f = t
    tile_b = jnp.where(t == 0, 0, n_tiles - t)

    def prep(tile, z_col, lb_row, g_ref, h_ref, forward):
        rows = pl.ds(pl.multiple_of(tile * TM, TM), TM)
        pos = lax.broadcasted_iota(jnp.int32, (TM, W), 0) & (C - 1)
        lf, lk = _hgrn_gates(hg_ref[rows, z_col * W:(z_col + 1) * W],
                             lb_ref[lb_row:lb_row + 1, :], lb_ref[lb_row + 1:lb_row + 2, :])
        for s in (1, 2, 4, 8):
            if forward:
                lf = lf + jnp.where(pos >= s, pltpu.roll(lf, s, 0), 0.0)
            else:
                lf = lf + jnp.where(pos < C - s, pltpu.roll(lf, TM - s, 0), 0.0)
        g_ref[rows, :] = lf
        h_ref[rows, :] = lf - lk

    def by_head(x):
        return jnp.concatenate([jnp.where(lane_head == hh, x, 0.0) for hh in range(HG_HEADS)], axis=0)

    def chunk_local(n, g_ref, h_ref, forward):
        rows = pl.ds(pl.multiple_of(n * C, C), C)
        g = g_ref[rows, :]
        h = h_ref[rows, :]
        q = hg_ref[rows, 0:W]
        v = hg_ref[rows, W:2 * W]
        g_end = g[C - 1:C, :] if forward else g[0:1, :]
        q_dec = by_head(q * jnp.exp2(g)).astype(BF16)
        parts, where = [], []
        for s in range(C):
            hs = h[s:s + 1, :]
            for t0 in range(0, C, SUB):
                if (forward and t0 + SUB - 1 < s) or (not forward and t0 > s):
                    continue
                decay = jnp.exp2(g[t0:t0 + SUB, :] - hs)
                if t0 <= s < t0 + SUB:
                    causal = (tpos >= s - t0) if forward else (tpos <= s - t0)
                    decay = jnp.where(causal, decay, 0.0)
                parts.append(q[t0:t0 + SUB, :] * decay)
                where.append((s, t0))
        a = _dot(jnp.concatenate(parts, axis=0).astype(BF16), ones_bd)
        o_tiles = {t0: None for t0 in range(0, C, SUB)}
        for i, (s, t0) in enumerate(where):
            term = a[i * SUB:(i + 1) * SUB, :] * v[s:s + 1, :]
            o_tiles[t0] = term if o_tiles[t0] is None else o_tiles[t0] + term
        o_intra = jnp.concatenate([o_tiles[t0] for t0 in range(0, C, SUB)], axis=0)
        k_dec = by_head(jnp.exp2(g_end - h)).astype(BF16)
        v_rows = jnp.concatenate([v[:, hh * HG_DK:(hh + 1) * HG_DK] for hh in range(HG_HEADS)], axis=0)
        kv = lax.dot_general(v_rows.astype(BF16), k_dec, (((0,), (0,)), ((), ())),
                             preferred_element_type=F32)
        return rows, q_dec, o_intra, jnp.exp2(g_end), kv

    def chunk_carry(st, local, out_ref):
        rows, q_dec, o_intra, a_end, kv = local
        r = _dot_nt(q_dec, st.astype(BF16))
        o_inter = jnp.concatenate([r[hh * C:(hh + 1) * C, :] for hh in range(HG_HEADS)], axis=1)
        out_ref[rows, :] = o_inter + o_intra
        return st * a_end + kv

    def run(n_keys, first_carried):
        prep(tile_f, 2, 0, gf_ref, hf_ref, True)
        prep(tile_b, 3, 2, gb_ref, hb_ref, False)
        st_f = sf_ref[...]
        st_b = sb_ref[...]
        pending = None
        blocks = [(j, slice(r * ATTN_ROWS, (r + 1) * ATTN_ROWS))
                  for j in range(N_HEADS) for r in range(TM // ATTN_ROWS)]
        assert len(blocks) == HG_TILE_CHUNKS
        scores = lambda ref, blk, nk: _dot_nt(ref[blk[0], blk[1], :], k_ref[blk[0] // Q_PER_KV, 0:nk, :])
        s_next = s_ref[...] if first_carried else scores(q_ref, blocks[0], n_keys)
        for i, (j, rows) in enumerate(blocks):
            s = s_next
            if i + 1 < len(blocks):
                s_next = scores(q_ref, blocks[i + 1], n_keys)
            else:
                s_ref[...] = scores(qn_ref, blocks[0], T_ALL)
            m = jnp.max(s, axis=-1, keepdims=True)
            p = jnp.exp2(s - m).astype(BF16)
            acc = _dot(p, v_ref[j // Q_PER_KV, 0:n_keys, :])
            o = acc / pltpu.roll(acc, HEAD_DIM, 1)
            attn_ref[rows, j * HEAD_DIM:(j + 1) * HEAD_DIM] = o[:, 0:HEAD_DIM].astype(BF16)

            local_f = chunk_local(tile_f * HG_TILE_CHUNKS + i, gf_ref, hf_ref, True)
            local_b = chunk_local(tile_b * HG_TILE_CHUNKS + (HG_TILE_CHUNKS - 1 - i), gb_ref, hb_ref, False)
            if pending is not None:
                st_f = chunk_carry(st_f, pending[0], of_ref)
                st_b = chunk_carry(st_b, pending[1], ob_ref)
            pending = (local_f, local_b)
        sf_ref[...] = chunk_carry(st_f, pending[0], of_ref)
        sb_ref[...] = chunk_carry(st_b, pending[1], ob_ref)

    @pl.when(t == 0)
    def _():
        run(CTX_LEN, False)

    @pl.when(t > 0)
    def _():
        run(T_ALL, True)

    @pl.when(t == n_tiles - 1)
    def _():
        def readout(i, carry):
            rows = pl.ds(pl.multiple_of(i * TM, TM), TM)
            o = of_ref[rows, :] + ob_ref[rows, :]
            y = o * lax.rsqrt(_head_mean_sq(o, ones_bd) + NORM_EPS) * nw_ref[...]
            gate = hg_ref[rows, 4 * W:5 * W]
            hgo_ref[rows, :] = (y * (gate * _sigmoid(gate))).astype(BF16)
            return carry

        lax.fori_loop(0, T_ALL // TM, readout, 0)


def _mixer_call(q, k, v, hg5, lb_tab, nw, ones_bd):
    nb = q.shape[0]
    n_tiles = T_ALL // TM
    kv_spec = lambda w: pl.BlockSpec((None, N_KV_HEADS, T_ALL, w), lambda b, t: (b, 0, 0, 0))
    q_spec = lambda nxt: pl.BlockSpec((None, N_HEADS, TM, HEAD_DIM),
                                      lambda b, t: (b, 0, jnp.minimum(t + nxt, n_tiles - 1), 0))
    batch_spec = lambda w: pl.BlockSpec((None, T_ALL, w), lambda b, t: (b, 0, 0))
    seq = lambda: pltpu.VMEM((T_ALL, HG_WIDTH), F32)
    state = lambda: pltpu.VMEM((HG_DK, HG_WIDTH), F32)
    return pl.pallas_call(
        _mixer_kernel,
        out_shape=[jax.ShapeDtypeStruct((nb, T_ALL, ATTN_WIDTH), BF16),
                   jax.ShapeDtypeStruct((nb, T_ALL, HG_WIDTH), BF16)],
        grid=(nb, n_tiles),
        in_specs=[q_spec(0), q_spec(1), kv_spec(HEAD_DIM), kv_spec(2 * HEAD_DIM), batch_spec(5 * HG_WIDTH),
                  _const_spec((4, HG_WIDTH)), _const_spec((1, HG_WIDTH)), _const_spec((HG_WIDTH, HG_WIDTH))],
        out_specs=[_tok_spec(ATTN_WIDTH), batch_spec(HG_WIDTH)],
        scratch_shapes=[pltpu.VMEM((ATTN_ROWS, T_ALL), F32), seq(), seq(), seq(), seq(), seq(), seq(),
                        state(), state()],
        compiler_params=_params(2),
        name="attn_hgrn",
    )(q, q, k, v, hg5, lb_tab, nw, ones_bd)


def _mix_ffn_kernel(pool_ref, attn_ref, hgo_ref, gate_ref, x_ref, mod_ref, nw_ref,
                    wbp_ref, wba_ref, wbh_ref, wo_ref, w1_ref, w2_ref, o_ref):
    y = (gate_ref[:, 0:D_MODEL].astype(F32) * _dot(pool_ref[...], wbp_ref[...])
         + gate_ref[:, D_MODEL:2 * D_MODEL].astype(F32) * _dot(attn_ref[...], wba_ref[...])
         + gate_ref[:, 2 * D_MODEL:3 * D_MODEL].astype(F32) * _dot(hgo_ref[...], wbh_ref[...]))
    x = x_ref[...] + mod_ref[2:3, :] * _dot(y.astype(BF16), wo_ref[...])
    h = _rms_mod(x, nw_ref[...], mod_ref[4:5, :], mod_ref[3:4, :]).astype(BF16)
    a = _dot(h, w1_ref[:, 0:D_FF])
    b = _dot(h, w1_ref[:, D_FF:2 * D_FF])
    s = (a * _sigmoid(a) * b).astype(BF16)
    o_ref[...] = x + mod_ref[5:6, :] * _dot(s, w2_ref[...])


def _mix_ffn_call(pool_o, attn_o, hg_o, gates, xs, mods, nw, wbp, wba, wbh, wo, w1, w2, l, latent_only):
    nb = xs.shape[0]
    t_off = 1 if latent_only else 0
    n_rows = SEQ if latent_only else T_ALL
    return pl.pallas_call(
        _mix_ffn_kernel,
        out_shape=jax.ShapeDtypeStruct((nb, n_rows, D_MODEL), F32),
        grid=(nb, n_rows // TM),
        in_specs=[_tok_spec(POOL_WIDTH, t_off), _tok_spec(ATTN_WIDTH, t_off), _tok_spec(HG_WIDTH, t_off),
                  _tok_spec(3 * D_MODEL, t_off), _tok_spec(D_MODEL, t_off), _mod_spec(l, t_off),
                  _layer_spec((1, D_MODEL), l),
                  _layer_spec((POOL_WIDTH, D_MODEL), l), _layer_spec((ATTN_WIDTH, D_MODEL), l),
                  _layer_spec((HG_WIDTH, D_MODEL), l), _layer_spec((D_MODEL, D_MODEL), l),
                  _layer_spec((D_MODEL, 2 * D_FF), l), _layer_spec((D_FF, D_MODEL), l)],
        out_specs=_tok_spec(D_MODEL),
        compiler_params=_params(2),
        name="mix_ffn",
    )(pool_o, attn_o, hg_o, gates, xs, mods, nw, wbp, wba, wbh, wo, w1, w2)


def _rope_tables():
    t = jnp.arange(SEQ, dtype=jnp.int32)
    freqs = ROPE_THETA ** (-jnp.arange(ROPE_HALF, dtype=F32) * (2.0 / (2 * ROPE_HALF)))
    cos_parts, sin_parts = [], []
    for pos in (t // GRID_W, t % GRID_W):
        ang = pos.astype(F32)[:, None] * freqs[None, :]
        cos_parts += [jnp.cos(ang), jnp.cos(ang)]
        sin_parts += [-jnp.sin(ang), jnp.sin(ang)]
    cos = jnp.concatenate(cos_parts, axis=1)
    sin = jnp.concatenate(sin_parts, axis=1)
    cos = jnp.concatenate([jnp.ones((CTX_LEN, HEAD_DIM), F32), cos], axis=0)
    sin = jnp.concatenate([jnp.zeros((CTX_LEN, HEAD_DIM), F32), sin], axis=0)
    return jnp.tile(cos, (1, N_KV_HEADS)), jnp.tile(sin, (1, N_KV_HEADS))


def _block_diag_ones(n):
    i = jnp.arange(n) // HEAD_DIM
    return (i[:, None] == i[None, :]).astype(BF16)


def _pool_block_diag(pool_w_l):
    w = jnp.zeros((POOL_WIDTH, POOL_WIDTH), F32)
    for g in range(POOL_WIDTH // POOL_GROUP):
        sl = slice(g * POOL_GROUP, (g + 1) * POOL_GROUP)
        w = w.at[sl, sl].set(pool_w_l[g])
    return w.astype(BF16)


def kernel(x, c, ctx, c_ctx, w_ada, b_ada, norm1_w, w_in, pool_w, pool_scale, q_norm_w, k_norm_w,
           hg_lb_logits, hg_norm_w, w_branch_pool, w_branch_attn, w_branch_hg, w_out, norm2_w,
           w_ffn_in, w_ffn_out):
    lb_cum = jnp.cumsum(jax.nn.softmax(hg_lb_logits.astype(F32), axis=0), axis=0)
    lb_all = lb_cum - lb_cum[0]
    lb_tab = jnp.stack([jnp.log(lb_all), jnp.log1p(-lb_all)], axis=2) * LOG2_E
    lb_tab = lb_tab.reshape(DEPTH, 4, HG_WIDTH)

    cos_t, sin_t = _rope_tables()
    ones_bd = _block_diag_ones(ONES_WIDTH)

    bf = lambda w: w.astype(BF16)
    w_in, wbp, wba, wbh, wo = bf(w_in), bf(w_branch_pool), bf(w_branch_attn), bf(w_branch_hg), bf(w_out)
    w1, w2 = bf(w_ffn_in), bf(w_ffn_out)
    nw1 = norm1_w.reshape(DEPTH, 1, D_MODEL)
    nw2 = norm2_w.reshape(DEPTH, 1, D_MODEL)
    qw = jnp.tile(q_norm_w, (1, N_HEADS)).reshape(DEPTH, 1, ATTN_WIDTH)
    kw = jnp.tile(k_norm_w, (1, N_KV_HEADS)).reshape(DEPTH, 1, KV_WIDTH)
    hw = jnp.tile(hg_norm_w, (1, HG_HEADS)).reshape(DEPTH, 1, HG_WIDTH)

    cc = jnp.zeros((MOD_ROWS, D_MODEL), F32).at[:BATCH].set(c).at[BATCH].set(c_ctx)
    mods = _ada_call(cc, w_ada, b_ada).reshape(DEPTH, MOD_ROWS, N_MOD, D_MODEL)

    xs = jnp.concatenate([ctx, x], axis=1)
    for l in range(DEPTH):
        pool_u, q, k, v, hg5, gates = _inproj_call(xs, mods, nw1, w_in, qw, kw, cos_t, sin_t, ones_bd, l)
        pool_o = _pool_call(pool_u, _pool_block_diag(pool_w[l]), pool_scale[l].reshape(1, POOL_WIDTH))
        attn_o, hg_o = _mixer_call(q, k, v, hg5, lb_tab[l], hw[l], ones_bd)
        xs = _mix_ffn_call(pool_o, attn_o, hg_o, gates, xs, mods, nw2, wbp, wba, wbh, wo, w1, w2, l,
                           latent_only=(l == DEPTH - 1))
    return xs
```

```python
import jax
import jax.numpy as jnp
from jax import lax
from jax.experimental import pallas as pl
from jax.experimental.pallas import tpu as pltpu

D_MODEL = 1024
BATCH = 8
SEQ = 2048
DEPTH = 4
GRID_W = 64
CTX_LEN = 256
T_ALL = CTX_LEN + SEQ
NORM_EPS = 1e-6
N_MOD = 6

POOL_GROUP = 64
POOL_WIDTH = 256
N_HEADS = 8
N_KV_HEADS = 2
HEAD_DIM = 64
Q_PER_KV = N_HEADS // N_KV_HEADS
ATTN_WIDTH = N_HEADS * HEAD_DIM
KV_WIDTH = N_KV_HEADS * HEAD_DIM
ROPE_THETA = 10000.0
ROPE_HALF = HEAD_DIM // 4
LOG2_E = 1.4426950408889634
ONES_WIDTH = 256
HG_HEADS = 4
HG_DK = 64
HG_WIDTH = 256
HG_CHUNK = 16
D_FF = 2816
IN_WIDTH = 5376

C_POOL = 0
C_QKV = 256
C_HG = 1024
C_GATE = 2304

MOD_ROWS = 16
TM = 256
VMEM_LIMIT = 56 * 1024 * 1024

BF16 = jnp.bfloat16
F32 = jnp.float32


def _dot(a, b):
    return jnp.dot(a, b, preferred_element_type=F32)


def _dot_nt(a, b):
    return lax.dot_general(a, b, (((1,), (1,)), ((), ())), preferred_element_type=F32)


def _sigmoid(x):
    return 1.0 / (1.0 + jnp.exp(-x))


def _params(n_grid):
    return pltpu.CompilerParams(dimension_semantics=("arbitrary",) * n_grid,
                                vmem_limit_bytes=VMEM_LIMIT)


def _ada_kernel(c_ref, w_ref, b_ref, o_ref):
    c = c_ref[...]
    act = (c * _sigmoid(c)).astype(BF16)
    o_ref[...] = _dot(act, w_ref[...].astype(BF16)) + b_ref[...]


def _ada_call(cc, w_ada, b_ada):
    tn = 1536
    n = N_MOD * D_MODEL
    return pl.pallas_call(
        _ada_kernel,
        out_shape=jax.ShapeDtypeStruct((DEPTH, MOD_ROWS, n), F32),
        grid=(DEPTH, n // tn),
        in_specs=[
            pl.BlockSpec((MOD_ROWS, D_MODEL), lambda l, j: (0, 0)),
            pl.BlockSpec((None, D_MODEL, tn), lambda l, j: (l, 0, j)),
            pl.BlockSpec((None, 1, tn), lambda l, j: (l, 0, j)),
        ],
        out_specs=pl.BlockSpec((None, MOD_ROWS, tn), lambda l, j: (l, 0, j)),
        compiler_params=_params(2),
        name="ada_mod",
    )(cc, w_ada, b_ada.reshape(DEPTH, 1, n))


def _mod_spec(l, t_off=0):
    return pl.BlockSpec((None, None, N_MOD, D_MODEL),
                        lambda b, t: (l, jnp.where(t + t_off == 0, BATCH, b), 0, 0))


def _const_spec(shape):
    nd = len(shape)
    return pl.BlockSpec(shape, lambda *_: (0,) * nd, pipeline_mode=pl.Buffered(1))


def _layer_spec(shape, l):
    nd = len(shape)
    return pl.BlockSpec((None,) + tuple(shape), lambda *_: (l,) + (0,) * nd,
                        pipeline_mode=pl.Buffered(1))


def _tok_spec(width, t_off=0):
    return pl.BlockSpec((None, TM, width), lambda b, t: (b, t + t_off, 0))


def _rms_mod(x, nw, scale, shift):
    ms = jnp.mean(x * x, axis=-1, keepdims=True)
    y = x * lax.rsqrt(ms + NORM_EPS) * nw
    return y * (1.0 + scale) + shift


def _head_mean_sq(x, ones_bd):
    bw = ones_bd.shape[0]
    if x.shape[1] > bw:
        return jnp.concatenate([_head_mean_sq(x[:, i:i + bw], ones_bd) for i in range(0, x.shape[1], bw)],
                               axis=1)
    x2 = x * x
    hi = x2.astype(BF16)
    lo = (x2 - hi.astype(F32)).astype(BF16)
    return (_dot(hi, ones_bd) + _dot(lo, ones_bd)) * (1.0 / HEAD_DIM)


def _swap_rope_halves(x):
    w = x.shape[1]
    lane = lax.broadcasted_iota(jnp.int32, x.shape, 1)
    up = pltpu.roll(x, w - ROPE_HALF, 1)
    dn = pltpu.roll(x, ROPE_HALF, 1)
    return jnp.where((lane & ROPE_HALF) == 0, up, dn)


def _inproj_kernel(x0_ref, mod0_ref, xn_ref, modn_ref, nw_ref, w_ref, qw_ref, kw_ref, cos_ref, sin_ref, ones_ref,
                   pool_ref, q_ref, k_ref, v_ref, hg_ref, gate_ref, ha_ref, hb_ref):
    step = pl.program_id(0) * pl.num_programs(1) + pl.program_id(1)

    @pl.when(step == 0)
    def _():
        ha_ref[...] = _rms_mod(x0_ref[...], nw_ref[...], mod0_ref[1:2, :], mod0_ref[0:1, :]).astype(BF16)

    @pl.when(step % 2 == 0)
    def _():
        _inproj_body(ha_ref, hb_ref, xn_ref, modn_ref, nw_ref, w_ref, qw_ref, kw_ref, cos_ref, sin_ref, ones_ref,
                     pool_ref, q_ref, k_ref, v_ref, hg_ref, gate_ref)

    @pl.when(step % 2 == 1)
    def _():
        _inproj_body(hb_ref, ha_ref, xn_ref, modn_ref, nw_ref, w_ref, qw_ref, kw_ref, cos_ref, sin_ref, ones_ref,
                     pool_ref, q_ref, k_ref, v_ref, hg_ref, gate_ref)


def _inproj_body(h_ref, hn_ref, xn_ref, modn_ref, nw_ref, w_ref, qw_ref, kw_ref, cos_ref, sin_ref, ones_ref,
                 pool_ref, q_ref, k_ref, v_ref, hg_ref, gate_ref):
    h = h_ref[...]
    pool_ref[...] = _dot(h, w_ref[:, C_POOL:C_QKV])
    hn_ref[...] = _rms_mod(xn_ref[...], nw_ref[...], modn_ref[1:2, :], modn_ref[0:1, :]).astype(BF16)

    qkv = _dot(h, w_ref[:, C_QKV:C_HG])
    cos = cos_ref[...]
    sin = sin_ref[...]
    q = qkv[:, 0:ATTN_WIDTH]
    q = q * lax.rsqrt(_head_mean_sq(q, ones_ref[...]) + NORM_EPS) * qw_ref[...]
    cos_q = jnp.concatenate([cos] * (ATTN_WIDTH // KV_WIDTH), axis=1)
    sin_q = jnp.concatenate([sin] * (ATTN_WIDTH // KV_WIDTH), axis=1)
    q = (q * cos_q + _swap_rope_halves(q) * sin_q) * (HEAD_DIM ** -0.5 * LOG2_E)
    for hh in range(N_HEADS):
        q_ref[hh] = q[:, hh * HEAD_DIM:(hh + 1) * HEAD_DIM].astype(BF16)
    k = qkv[:, ATTN_WIDTH:ATTN_WIDTH + KV_WIDTH]
    k = k * lax.rsqrt(_head_mean_sq(k, ones_ref[0:KV_WIDTH, 0:KV_WIDTH]) + NORM_EPS) * kw_ref[...]
    k = k * cos + _swap_rope_halves(k) * sin
    v = qkv[:, ATTN_WIDTH + KV_WIDTH:]
    ones = jnp.ones((TM, HEAD_DIM), F32)
    for hh in range(N_KV_HEADS):
        k_ref[hh] = k[:, hh * HEAD_DIM:(hh + 1) * HEAD_DIM].astype(BF16)
        v_ref[hh] = jnp.concatenate([v[:, hh * HEAD_DIM:(hh + 1) * HEAD_DIM], ones], axis=1).astype(BF16)

    hg_ref[...] = _dot(h, w_ref[:, C_HG:C_GATE])
    gate_ref[...] = _sigmoid(_dot(h, w_ref[:, C_GATE:IN_WIDTH])).astype(BF16)


def _inproj_call(xs, mods, nw, w_in, qw, kw, cos_t, sin_t, ones_bd, l):
    nb = xs.shape[0]
    head_spec = lambda nh, w: pl.BlockSpec((None, nh, TM, w), lambda b, t: (b, 0, t, 0))
    rope_spec = pl.BlockSpec((TM, KV_WIDTH), lambda b, t: (t, 0))
    out_shape = [
        jax.ShapeDtypeStruct((nb, T_ALL, POOL_WIDTH), F32),
        jax.ShapeDtypeStruct((nb, N_HEADS, T_ALL, HEAD_DIM), BF16),
        jax.ShapeDtypeStruct((nb, N_KV_HEADS, T_ALL, HEAD_DIM), BF16),
        jax.ShapeDtypeStruct((nb, N_KV_HEADS, T_ALL, 2 * HEAD_DIM), BF16),
        jax.ShapeDtypeStruct((nb, T_ALL, C_GATE - C_HG), F32),
        jax.ShapeDtypeStruct((nb, T_ALL, IN_WIDTH - C_GATE), BF16),
    ]
    out_specs = [_tok_spec(POOL_WIDTH), head_spec(N_HEADS, HEAD_DIM), head_spec(N_KV_HEADS, HEAD_DIM),
                 head_spec(N_KV_HEADS, 2 * HEAD_DIM), _tok_spec(C_GATE - C_HG), _tok_spec(IN_WIDTH - C_GATE)]
    n_tiles = T_ALL // TM

    def next_tile(b, t):
        wrap = (t + 1) // n_tiles
        return jnp.minimum(b + wrap, nb - 1), (t + 1) % n_tiles

    def xn_map(b, t):
        bn, tn = next_tile(b, t)
        return bn, tn, 0

    def modn_map(b, t):
        bn, tn = next_tile(b, t)
        return l, jnp.where(tn == 0, BATCH, bn), 0, 0

    return pl.pallas_call(
        _inproj_kernel,
        out_shape=out_shape,
        grid=(nb, n_tiles),
        in_specs=[pl.BlockSpec((None, TM, D_MODEL), lambda b, t: (0, 0, 0)),
                  pl.BlockSpec((None, None, N_MOD, D_MODEL), lambda b, t: (l, BATCH, 0, 0)),
                  pl.BlockSpec((None, TM, D_MODEL), xn_map),
                  pl.BlockSpec((None, None, N_MOD, D_MODEL), modn_map),
                  _layer_spec((1, D_MODEL), l),
                  _layer_spec((D_MODEL, IN_WIDTH), l), _layer_spec((1, ATTN_WIDTH), l),
                  _layer_spec((1, KV_WIDTH), l), rope_spec, rope_spec,
                  _const_spec((ONES_WIDTH, ONES_WIDTH))],
        out_specs=out_specs,
        scratch_shapes=[pltpu.VMEM((TM, D_MODEL), BF16), pltpu.VMEM((TM, D_MODEL), BF16)],
        compiler_params=_params(2),
        name="in_proj",
    )(xs, mods, xs, mods, nw, w_in, qw, kw, cos_t, sin_t, ones_bd)


def _pool_segment(u, w_bd, scale):
    n = u.shape[0]
    row = lax.broadcasted_iota(jnp.int32, u.shape, 0)
    grp = lax.broadcasted_iota(jnp.int32, u.shape, 1) // POOL_GROUP

    def prev(x, s):
        return jnp.where(row >= s, pltpu.roll(x, s, 0), 0.0)

    def nxt(x, s):
        return jnp.where(row < n - s, pltpu.roll(x, n - s, 0), 0.0)

    fwd = [u]
    bwd = [prev(u, 1)]
    for j in range(3):
        s = 1 << j
        fwd.append(fwd[j] + nxt(fwd[j], s))
        bwd.append(bwd[j] + prev(bwd[j], s))
    win = fwd[3] + bwd[3]
    for j in (2, 1, 0):
        win = jnp.where(grp == j, fwd[j] + bwd[j], win)
    half = jnp.left_shift(1, grp)
    cnt = jnp.minimum(row + half, n) - jnp.maximum(row - half, 0)
    y = win / cnt.astype(F32) - u
    return (_dot(y.astype(BF16), w_bd) * scale).astype(BF16)


def _pool_kernel(u_ref, w_ref, s_ref, o_ref):
    o_ref[0:CTX_LEN, :] = _pool_segment(u_ref[0:CTX_LEN, :], w_ref[...], s_ref[...])
    o_ref[CTX_LEN:T_ALL, :] = _pool_segment(u_ref[CTX_LEN:T_ALL, :], w_ref[...], s_ref[...])


def _pool_call(pool_u, w_bd, scale):
    nb = pool_u.shape[0]
    spec = pl.BlockSpec((None, T_ALL, POOL_WIDTH), lambda b: (b, 0, 0))
    return pl.pallas_call(
        _pool_kernel,
        out_shape=jax.ShapeDtypeStruct((nb, T_ALL, POOL_WIDTH), BF16),
        grid=(nb,),
        in_specs=[spec, _const_spec((POOL_WIDTH, POOL_WIDTH)), _const_spec((1, POOL_WIDTH))],
        out_specs=spec,
        compiler_params=_params(1),
        name="pool_mix",
    )(pool_u, w_bd, scale)


def _hgrn_gates(z, log_lb, log_1m_lb):
    z = z * LOG2_E
    log_sig = jnp.minimum(z, 0.0) - jnp.log2(1.0 + jnp.exp2(-jnp.abs(z)))
    b = log_1m_lb + log_sig
    log_f = jnp.maximum(log_lb, b) + jnp.log2(1.0 + jnp.exp2(-jnp.abs(log_lb - b)))
    return log_f, b - z


HG_TILE_CHUNKS = TM // HG_CHUNK
ATTN_ROWS = 128


def _mixer_kernel(q_ref, qn_ref, k_ref, v_ref, hg_ref, lb_ref, nw_ref, ones_ref, attn_ref, hgo_ref,
                  s_ref, gf_ref, hf_ref, gb_ref, hb_ref, of_ref, ob_ref, sf_ref, sb_ref):
    C = HG_CHUNK
    W = HG_WIDTH
    SUB = 8
    t = pl.program_id(1)
    n_tiles = pl.num_programs(1)
    ones_bd = ones_ref[...]
    tpos = lax.broadcasted_iota(jnp.int32, (SUB, W), 0)
    lane_head = lax.broadcasted_iota(jnp.int32, (C, W), 1) // HG_DK

    @pl.when(t == 0)
    def _():
        sf_ref[...] = jnp.zeros((HG_DK, W), F32)
        sb_ref[...] = jnp.zeros((HG_DK, W), F32)

    tile_f = t
    tile_b = jnp.where(t == 0, 0, n_tiles - t)

    def prep(tile, z_col, lb_row, g_ref, h_ref, forward):
        rows = pl.ds(pl.multiple_of(tile * TM, TM), TM)
        pos = lax.broadcasted_iota(jnp.int32, (TM, W), 0) & (C - 1)
        lf, lk = _hgrn_gates(hg_ref[rows, z_col * W:(z_col + 1) * W],
                             lb_ref[lb_row:lb_row + 1, :], lb_ref[lb_row + 1:lb_row + 2, :])
        for s in (1, 2, 4, 8):
            if forward:
                lf = lf + jnp.where(pos >= s, pltpu.roll(lf, s, 0), 0.0)
            else:
                lf = lf + jnp.where(pos < C - s, pltpu.roll(lf, TM - s, 0), 0.0)
        g_ref[rows, :] = lf
        h_ref[rows, :] = lf - lk

    def by_head(x):
        return jnp.concatenate([jnp.where(lane_head == hh, x, 0.0) for hh in range(HG_HEADS)], axis=0)

    def chunk_local(n, g_ref, h_ref, forward):
        rows = pl.ds(pl.multiple_of(n * C, C), C)
        g = g_ref[rows, :]
        h = h_ref[rows, :]
        q = hg_ref[rows, 0:W]
        v = hg_ref[rows, W:2 * W]
        g_end = g[C - 1:C, :] if forward else g[0:1, :]
        q_dec = by_head(q * jnp.exp2(g)).astype(BF16)
        parts, where = [], []
        for s in range(C):
            hs = h[s:s + 1, :]
            for t0 in range(0, C, SUB):
                if (forward and t0 + SUB - 1 < s) or (not forward and t0 > s):
                    continue
                decay = jnp.exp2(g[t0:t0 + SUB, :] - hs)
                if t0 <= s < t0 + SUB:
                    causal = (tpos >= s - t0) if forward else (tpos <= s - t0)
                    decay = jnp.where(causal, decay, 0.0)
                parts.append(q[t0:t0 + SUB, :] * decay)
                where.append((s, t0))
        a = _dot(jnp.concatenate(parts, axis=0).astype(BF16), ones_bd)
        o_tiles = {t0: None for t0 in range(0, C, SUB)}
        for i, (s, t0) in enumerate(where):
            term = a[i * SUB:(i + 1) * SUB, :] * v[s:s + 1, :]
            o_tiles[t0] = term if o_tiles[t0] is None else o_tiles[t0] + term
        o_intra = jnp.concatenate([o_tiles[t0] for t0 in range(0, C, SUB)], axis=0)
        k_dec = by_head(jnp.exp2(g_end - h)).astype(BF16)
        v_rows = jnp.concatenate([v[:, hh * HG_DK:(hh + 1) * HG_DK] for hh in range(HG_HEADS)], axis=0)
        kv = lax.dot_general(v_rows.astype(BF16), k_dec, (((0,), (0,)), ((), ())),
                             preferred_element_type=F32)
        return rows, q_dec, o_intra, jnp.exp2(g_end), kv

    def chunk_carry(st, local, out_ref):
        rows, q_dec, o_intra, a_end, kv = local
        r = _dot_nt(q_dec, st.astype(BF16))
        o_inter = jnp.concatenate([r[hh * C:(hh + 1) * C, :] for hh in range(HG_HEADS)], axis=1)
        out_ref[rows, :] = o_inter + o_intra
        return st * a_end + kv

    def run(n_keys, first_carried):
        prep(tile_f, 2, 0, gf_ref, hf_ref, True)
        prep(tile_b, 3, 2, gb_ref, hb_ref, False)
        st_f = sf_ref[...]
        st_b = sb_ref[...]
        pending = None
        blocks = [(j, slice(r * ATTN_ROWS, (r + 1) * ATTN_ROWS))
                  for j in range(N_HEADS) for r in range(TM // ATTN_ROWS)]
        assert len(blocks) == HG_TILE_CHUNKS
        scores = lambda ref, blk, nk: _dot_nt(ref[blk[0], blk[1], :], k_ref[blk[0] // Q_PER_KV, 0:nk, :])
        s_next = s_ref[...] if first_carried else scores(q_ref, blocks[0], n_keys)
        for i, (j, rows) in enumerate(blocks):
            s = s_next
            if i + 1 < len(blocks):
                s_next = scores(q_ref, blocks[i + 1], n_keys)
            else:
                s_ref[...] = scores(qn_ref, blocks[0], T_ALL)
            m = jnp.max(s, axis=-1, keepdims=True)
            p = jnp.exp2(s - m).astype(BF16)
            acc = _dot(p, v_ref[j // Q_PER_KV, 0:n_keys, :])
            o = acc / pltpu.roll(acc, HEAD_DIM, 1)
            attn_ref[rows, j * HEAD_DIM:(j + 1) * HEAD_DIM] = o[:, 0:HEAD_DIM].astype(BF16)

            local_f = chunk_local(tile_f * HG_TILE_CHUNKS + i, gf_ref, hf_ref, True)
            local_b = chunk_local(tile_b * HG_TILE_CHUNKS + (HG_TILE_CHUNKS - 1 - i), gb_ref, hb_ref, False)
            if pending is not None:
                st_f = chunk_carry(st_f, pending[0], of_ref)
                st_b = chunk_carry(st_b, pending[1], ob_ref)
            pending = (local_f, local_b)
        sf_ref[...] = chunk_carry(st_f, pending[0], of_ref)
        sb_ref[...] = chunk_carry(st_b, pending[1], ob_ref)

    @pl.when(t == 0)
    def _():
        run(CTX_LEN, False)

    @pl.when(t > 0)
    def _():
        run(T_ALL, True)

    @pl.when(t == n_tiles - 1)
    def _():
        def readout(i, carry):
            rows = pl.ds(pl.multiple_of(i * TM, TM), TM)
            o = of_ref[rows, :] + ob_ref[rows, :]
            y = o * lax.rsqrt(_head_mean_sq(o, ones_bd) + NORM_EPS) * nw_ref[...]
            gate = hg_ref[rows, 4 * W:5 * W]
            hgo_ref[rows, :] = (y * (gate * _sigmoid(gate))).astype(BF16)
            return carry

        lax.fori_loop(0, T_ALL // TM, readout, 0)


def _mixer_call(q, k, v, hg5, lb_tab, nw, ones_bd):
    nb = q.shape[0]
    n_tiles = T_ALL // TM
    kv_spec = lambda w: pl.BlockSpec((None, N_KV_HEADS, T_ALL, w), lambda b, t: (b, 0, 0, 0))
    q_spec = lambda nxt: pl.BlockSpec((None, N_HEADS, TM, HEAD_DIM),
                                      lambda b, t: (b, 0, jnp.minimum(t + nxt, n_tiles - 1), 0))
    batch_spec = lambda w: pl.BlockSpec((None, T_ALL, w), lambda b, t: (b, 0, 0))
    seq = lambda: pltpu.VMEM((T_ALL, HG_WIDTH), F32)
    state = lambda: pltpu.VMEM((HG_DK, HG_WIDTH), F32)
    return pl.pallas_call(
        _mixer_kernel,
        out_shape=[jax.ShapeDtypeStruct((nb, T_ALL, ATTN_WIDTH), BF16),
                   jax.ShapeDtypeStruct((nb, T_ALL, HG_WIDTH), BF16)],
        grid=(nb, n_tiles),
        in_specs=[q_spec(0), q_spec(1), kv_spec(HEAD_DIM), kv_spec(2 * HEAD_DIM), batch_spec(5 * HG_WIDTH),
                  _const_spec((4, HG_WIDTH)), _const_spec((1, HG_WIDTH)), _const_spec((HG_WIDTH, HG_WIDTH))],
        out_specs=[_tok_spec(ATTN_WIDTH), batch_spec(HG_WIDTH)],
        scratch_shapes=[pltpu.VMEM((ATTN_ROWS, T_ALL), F32), seq(), seq(), seq(), seq(), seq(), seq(),
                        state(), state()],
        compiler_params=_params(2),
        name="attn_hgrn",
    )(q, q, k, v, hg5, lb_tab, nw, ones_bd)


def _mix_ffn_kernel(pool_ref, attn_ref, hgo_ref, gate_ref, x_ref, mod_ref, nw_ref,
                    wbp_ref, wba_ref, wbh_ref, wo_ref, w1_ref, w2_ref, o_ref):
    y = (gate_ref[:, 0:D_MODEL].astype(F32) * _dot(pool_ref[...], wbp_ref[...])
         + gate_ref[:, D_MODEL:2 * D_MODEL].astype(F32) * _dot(attn_ref[...], wba_ref[...])
         + gate_ref[:, 2 * D_MODEL:3 * D_MODEL].astype(F32) * _dot(hgo_ref[...], wbh_ref[...]))
    x = x_ref[...] + mod_ref[2:3, :] * _dot(y.astype(BF16), wo_ref[...])
    h = _rms_mod(x, nw_ref[...], mod_ref[4:5, :], mod_ref[3:4, :]).astype(BF16)
    a = _dot(h, w1_ref[:, 0:D_FF])
    b = _dot(h, w1_ref[:, D_FF:2 * D_FF])
    s = (a * _sigmoid(a) * b).astype(BF16)
    o_ref[...] = x + mod_ref[5:6, :] * _dot(s, w2_ref[...])


def _mix_ffn_call(pool_o, attn_o, hg_o, gates, xs, mods, nw, wbp, wba, wbh, wo, w1, w2, l, latent_only):
    nb = xs.shape[0]
    t_off = 1 if latent_only else 0
    n_rows = SEQ if latent_only else T_ALL
    return pl.pallas_call(
        _mix_ffn_kernel,
        out_shape=jax.ShapeDtypeStruct((nb, n_rows, D_MODEL), F32),
        grid=(nb, n_rows // TM),
        in_specs=[_tok_spec(POOL_WIDTH, t_off), _tok_spec(ATTN_WIDTH, t_off), _tok_spec(HG_WIDTH, t_off),
                  _tok_spec(3 * D_MODEL, t_off), _tok_spec(D_MODEL, t_off), _mod_spec(l, t_off),
                  _layer_spec((1, D_MODEL), l),
                  _layer_spec((POOL_WIDTH, D_MODEL), l), _layer_spec((ATTN_WIDTH, D_MODEL), l),
                  _layer_spec((HG_WIDTH, D_MODEL), l), _layer_spec((D_MODEL, D_MODEL), l),
                  _layer_spec((D_MODEL, 2 * D_FF), l), _layer_spec((D_FF, D_MODEL), l)],
        out_specs=_tok_spec(D_MODEL),
        compiler_params=_params(2),
        name="mix_ffn",
    )(pool_o, attn_o, hg_o, gates, xs, mods, nw, wbp, wba, wbh, wo, w1, w2)


def _rope_tables():
    t = jnp.arange(SEQ, dtype=jnp.int32)
    freqs = ROPE_THETA ** (-jnp.arange(ROPE_HALF, dtype=F32) * (2.0 / (2 * ROPE_HALF)))
    cos_parts, sin_parts = [], []
    for pos in (t // GRID_W, t % GRID_W):
        ang = pos.astype(F32)[:, None] * freqs[None, :]
        cos_parts += [jnp.cos(ang), jnp.cos(ang)]
        sin_parts += [-jnp.sin(ang), jnp.sin(ang)]
    cos = jnp.concatenate(cos_parts, axis=1)
    sin = jnp.concatenate(sin_parts, axis=1)
    cos = jnp.concatenate([jnp.ones((CTX_LEN, HEAD_DIM), F32), cos], axis=0)
    sin = jnp.concatenate([jnp.zeros((CTX_LEN, HEAD_DIM), F32), sin], axis=0)
    return jnp.tile(cos, (1, N_KV_HEADS)), jnp.tile(sin, (1, N_KV_HEADS))


def _block_diag_ones(n):
    i = jnp.arange(n) // HEAD_DIM
    return (i[:, None] == i[None, :]).astype(BF16)


def _pool_block_diag(pool_w_l):
    w = jnp.zeros((POOL_WIDTH, POOL_WIDTH), F32)
    for g in range(POOL_WIDTH // POOL_GROUP):
        sl = slice(g * POOL_GROUP, (g + 1) * POOL_GROUP)
        w = w.at[sl, sl].set(pool_w_l[g])
    return w.astype(BF16)


def kernel(x, c, ctx, c_ctx, w_ada, b_ada, norm1_w, w_in, pool_w, pool_scale, q_norm_w, k_norm_w,
           hg_lb_logits, hg_norm_w, w_branch_pool, w_branch_attn, w_branch_hg, w_out, norm2_w,
           w_ffn_in, w_ffn_out):
    lb_cum = jnp.cumsum(jax.nn.softmax(hg_lb_logits.astype(F32), axis=0), axis=0)
    lb_all = lb_cum - lb_cum[0]
    lb_tab = jnp.stack([jnp.log(lb_all), jnp.log1p(-lb_all)], axis=2) * LOG2_E
    lb_tab = lb_tab.reshape(DEPTH, 4, HG_WIDTH)

    cos_t, sin_t = _rope_tables()
    ones_bd = _block_diag_ones(ONES_WIDTH)

    bf = lambda w: w.astype(BF16)
    w_in, wbp, wba, wbh, wo = bf(w_in), bf(w_branch_pool), bf(w_branch_attn), bf(w_branch_hg), bf(w_out)
    w1, w2 = bf(w_ffn_in), bf(w_ffn_out)
    nw1 = norm1_w.reshape(DEPTH, 1, D_MODEL)
    nw2 = norm2_w.reshape(DEPTH, 1, D_MODEL)
    qw = jnp.tile(q_norm_w, (1, N_HEADS)).reshape(DEPTH, 1, ATTN_WIDTH)
    kw = jnp.tile(k_norm_w, (1, N_KV_HEADS)).reshape(DEPTH, 1, KV_WIDTH)
    hw = jnp.tile(hg_norm_w, (1, HG_HEADS)).reshape(DEPTH, 1, HG_WIDTH)

    cc = jnp.zeros((MOD_ROWS, D_MODEL), F32).at[:BATCH].set(c).at[BATCH].set(c_ctx)
    mods = _ada_call(cc, w_ada, b_ada).reshape(DEPTH, MOD_ROWS, N_MOD, D_MODEL)

    xs = jnp.concatenate([ctx, x], axis=1)
    for l in range(DEPTH):
        pool_u, q, k, v, hg5, gates = _inproj_call(xs, mods, nw1, w_in, qw, kw, cos_t, sin_t, ones_bd, l)
        pool_o = _pool_call(pool_u, _pool_block_diag(pool_w[l]), pool_scale[l].reshape(1, POOL_WIDTH))
        attn_o, hg_o = _mixer_call(q, k, v, hg5, lb_tab[l], hw[l], ones_bd)
        xs = _mix_ffn_call(pool_o, attn_o, hg_o, gates, xs, mods, nw2, wbp, wba, wbh, wo, w1, w2, l,
                           latent_only=(l == DEPTH - 1))
    return xs
```

```python
import jax
import jax.numpy as jnp
from jax import lax
from jax.experimental import pallas as pl
from jax.experimental.pallas import tpu as pltpu

D_MODEL = 1024
BATCH = 8
SEQ = 2048
DEPTH = 4
GRID_W = 64
CTX_LEN = 256
T_ALL = CTX_LEN + SEQ
NORM_EPS = 1e-6
N_MOD = 6

POOL_GROUP = 64
POOL_WIDTH = 256
N_HEADS = 8
N_KV_HEADS = 2
HEAD_DIM = 64
Q_PER_KV = N_HEADS // N_KV_HEADS
ATTN_WIDTH = N_HEADS * HEAD_DIM
KV_WIDTH = N_KV_HEADS * HEAD_DIM
ROPE_THETA = 10000.0
ROPE_HALF = HEAD_DIM // 4
LOG2_E = 1.4426950408889634
ONES_WIDTH = 256
HG_HEADS = 4
HG_DK = 64
HG_WIDTH = 256
HG_CHUNK = 16
D_FF = 2816
IN_WIDTH = 5376

C_POOL = 0
C_QKV = 256
C_HG = 1024
C_GATE = 2304

MOD_ROWS = 16
TM = 256
VMEM_LIMIT = 56 * 1024 * 1024

BF16 = jnp.bfloat16
F32 = jnp.float32


def _dot(a, b):
    return jnp.dot(a, b, preferred_element_type=F32)


def _dot_nt(a, b):
    return lax.dot_general(a, b, (((1,), (1,)), ((), ())), preferred_element_type=F32)


def _sigmoid(x):
    return 1.0 / (1.0 + jnp.exp(-x))


def _params(n_grid):
    return pltpu.CompilerParams(dimension_semantics=("arbitrary",) * n_grid,
                                vmem_limit_bytes=VMEM_LIMIT)


def _ada_kernel(c_ref, w_ref, b_ref, o_ref):
    c = c_ref[...]
    act = (c * _sigmoid(c)).astype(BF16)
    o_ref[...] = _dot(act, w_ref[...].astype(BF16)) + b_ref[...]


def _ada_call(cc, w_ada, b_ada):
    tn = 1536
    n = N_MOD * D_MODEL
    return pl.pallas_call(
        _ada_kernel,
        out_shape=jax.ShapeDtypeStruct((DEPTH, MOD_ROWS, n), F32),
        grid=(DEPTH, n // tn),
        in_specs=[
            pl.BlockSpec((MOD_ROWS, D_MODEL), lambda l, j: (0, 0)),
            pl.BlockSpec((None, D_MODEL, tn), lambda l, j: (l, 0, j)),
            pl.BlockSpec((None, 1, tn), lambda l, j: (l, 0, j)),
        ],
        out_specs=pl.BlockSpec((None, MOD_ROWS, tn), lambda l, j: (l, 0, j)),
        compiler_params=_params(2),
        name="ada_mod",
    )(cc, w_ada, b_ada.reshape(DEPTH, 1, n))


def _mod_spec(l, t_off=0):
    return pl.BlockSpec((None, None, N_MOD, D_MODEL),
                        lambda b, t: (l, jnp.where(t + t_off == 0, BATCH, b), 0, 0))


def _const_spec(shape):
    nd = len(shape)
    return pl.BlockSpec(shape, lambda *_: (0,) * nd, pipeline_mode=pl.Buffered(1))


def _layer_spec(shape, l):
    nd = len(shape)
    return pl.BlockSpec((None,) + tuple(shape), lambda *_: (l,) + (0,) * nd,
                        pipeline_mode=pl.Buffered(1))


def _tok_spec(width, t_off=0):
    return pl.BlockSpec((None, TM, width), lambda b, t: (b, t + t_off, 0))


def _rms_mod(x, nw, scale, shift):
    ms = jnp.mean(x * x, axis=-1, keepdims=True)
    y = x * lax.rsqrt(ms + NORM_EPS) * nw
    return y * (1.0 + scale) + shift


def _head_mean_sq(x, ones_bd):
    bw = ones_bd.shape[0]
    if x.shape[1] > bw:
        return jnp.concatenate([_head_mean_sq(x[:, i:i + bw], ones_bd) for i in range(0, x.shape[1], bw)],
                               axis=1)
    x2 = x * x
    hi = x2.astype(BF16)
    lo = (x2 - hi.astype(F32)).astype(BF16)
    return (_dot(hi, ones_bd) + _dot(lo, ones_bd)) * (1.0 / HEAD_DIM)


def _swap_rope_halves(x):
    w = x.shape[1]
    lane = lax.broadcasted_iota(jnp.int32, x.shape, 1)
    up = pltpu.roll(x, w - ROPE_HALF, 1)
    dn = pltpu.roll(x, ROPE_HALF, 1)
    return jnp.where((lane & ROPE_HALF) == 0, up, dn)


def _stream_tile(x_refs):
    if len(x_refs) == 1:
        return x_refs[0][...]
    ctx_ref, lat_ref = x_refs
    return jnp.where(pl.program_id(1) == 0, ctx_ref[...], lat_ref[...])


def _stream_specs(split):
    if not split:
        return [_tok_spec(D_MODEL)]
    return [pl.BlockSpec((None, TM, D_MODEL), lambda b, t: (b, 0, 0)),
            pl.BlockSpec((None, TM, D_MODEL), lambda b, t: (b, jnp.maximum(t - 1, 0), 0))]


def _inproj_kernel(*refs):
    (mod_ref, nw_ref, w_ref, qw_ref, kw_ref, cos_ref, sin_ref, ones_ref,
     pool_ref, q_ref, k_ref, v_ref, hg_ref, gate_ref) = refs[-14:]
    h = _rms_mod(_stream_tile(refs[:-14]), nw_ref[...], mod_ref[1:2, :], mod_ref[0:1, :]).astype(BF16)
    pool_ref[...] = _dot(h, w_ref[:, C_POOL:C_QKV])

    qkv = _dot(h, w_ref[:, C_QKV:C_HG])
    cos = cos_ref[...]
    sin = sin_ref[...]
    q = qkv[:, 0:ATTN_WIDTH]
    q = q * lax.rsqrt(_head_mean_sq(q, ones_ref[...]) + NORM_EPS) * qw_ref[...]
    cos_q = jnp.concatenate([cos] * (ATTN_WIDTH // KV_WIDTH), axis=1)
    sin_q = jnp.concatenate([sin] * (ATTN_WIDTH // KV_WIDTH), axis=1)
    q = (q * cos_q + _swap_rope_halves(q) * sin_q) * (HEAD_DIM ** -0.5 * LOG2_E)
    for hh in range(N_HEADS):
        q_ref[hh] = q[:, hh * HEAD_DIM:(hh + 1) * HEAD_DIM].astype(BF16)
    k = qkv[:, ATTN_WIDTH:ATTN_WIDTH + KV_WIDTH]
    k = k * lax.rsqrt(_head_mean_sq(k, ones_ref[0:KV_WIDTH, 0:KV_WIDTH]) + NORM_EPS) * kw_ref[...]
    k = k * cos + _swap_rope_halves(k) * sin
    v = qkv[:, ATTN_WIDTH + KV_WIDTH:]
    ones = jnp.ones((TM, HEAD_DIM), F32)
    for hh in range(N_KV_HEADS):
        k_ref[hh] = k[:, hh * HEAD_DIM:(hh + 1) * HEAD_DIM].astype(BF16)
        v_ref[hh] = jnp.concatenate([v[:, hh * HEAD_DIM:(hh + 1) * HEAD_DIM], ones], axis=1).astype(BF16)

    hg_ref[...] = _dot(h, w_ref[:, C_HG:C_GATE])
    gate_ref[...] = _sigmoid(_dot(h, w_ref[:, C_GATE:IN_WIDTH])).astype(BF16)


def _inproj_call(stream, mods, nw, w_in, qw, kw, cos_t, sin_t, ones_bd, l):
    nb = stream[0].shape[0]
    head_spec = lambda nh, w: pl.BlockSpec((None, nh, TM, w), lambda b, t: (b, 0, t, 0))
    rope_spec = pl.BlockSpec((TM, KV_WIDTH), lambda b, t: (t, 0))
    out_shape = [
        jax.ShapeDtypeStruct((nb, T_ALL, POOL_WIDTH), F32),
        jax.ShapeDtypeStruct((nb, N_HEADS, T_ALL, HEAD_DIM), BF16),
        jax.ShapeDtypeStruct((nb, N_KV_HEADS, T_ALL, HEAD_DIM), BF16),
        jax.ShapeDtypeStruct((nb, N_KV_HEADS, T_ALL, 2 * HEAD_DIM), BF16),
        jax.ShapeDtypeStruct((nb, T_ALL, C_GATE - C_HG), F32),
        jax.ShapeDtypeStruct((nb, T_ALL, IN_WIDTH - C_GATE), BF16),
    ]
    out_specs = [_tok_spec(POOL_WIDTH), head_spec(N_HEADS, HEAD_DIM), head_spec(N_KV_HEADS, HEAD_DIM),
                 head_spec(N_KV_HEADS, 2 * HEAD_DIM), _tok_spec(C_GATE - C_HG), _tok_spec(IN_WIDTH - C_GATE)]
    return pl.pallas_call(
        _inproj_kernel,
        out_shape=out_shape,
        grid=(nb, T_ALL // TM),
        in_specs=_stream_specs(len(stream) == 2) + [
            _mod_spec(l), _layer_spec((1, D_MODEL), l),
            _layer_spec((D_MODEL, IN_WIDTH), l), _layer_spec((1, ATTN_WIDTH), l),
            _layer_spec((1, KV_WIDTH), l), rope_spec, rope_spec,
            _const_spec((ONES_WIDTH, ONES_WIDTH))],
        out_specs=out_specs,
        compiler_params=_params(2),
        name="in_proj",
    )(*stream, mods, nw, w_in, qw, kw, cos_t, sin_t, ones_bd)


def _pool_segment(u, w_bd, scale):
    n = u.shape[0]
    row = lax.broadcasted_iota(jnp.int32, u.shape, 0)
    grp = lax.broadcasted_iota(jnp.int32, u.shape, 1) // POOL_GROUP

    def prev(x, s):
        return jnp.where(row >= s, pltpu.roll(x, s, 0), 0.0)

    def nxt(x, s):
        return jnp.where(row < n - s, pltpu.roll(x, n - s, 0), 0.0)

    fwd = [u]
    bwd = [prev(u, 1)]
    for j in range(3):
        s = 1 << j
        fwd.append(fwd[j] + nxt(fwd[j], s))
        bwd.append(bwd[j] + prev(bwd[j], s))
    win = fwd[3] + bwd[3]
    for j in (2, 1, 0):
        win = jnp.where(grp == j, fwd[j] + bwd[j], win)
    half = jnp.left_shift(1, grp)
    cnt = jnp.minimum(row + half, n) - jnp.maximum(row - half, 0)
    y = win / cnt.astype(F32) - u
    return (_dot(y.astype(BF16), w_bd) * scale).astype(BF16)


def _pool_kernel(u_ref, w_ref, s_ref, o_ref):
    o_ref[0:CTX_LEN, :] = _pool_segment(u_ref[0:CTX_LEN, :], w_ref[...], s_ref[...])
    o_ref[CTX_LEN:T_ALL, :] = _pool_segment(u_ref[CTX_LEN:T_ALL, :], w_ref[...], s_ref[...])


def _pool_call(pool_u, w_bd, scale):
    nb = pool_u.shape[0]
    spec = pl.BlockSpec((None, T_ALL, POOL_WIDTH), lambda b: (b, 0, 0))
    return pl.pallas_call(
        _pool_kernel,
        out_shape=jax.ShapeDtypeStruct((nb, T_ALL, POOL_WIDTH), BF16),
        grid=(nb,),
        in_specs=[spec, _const_spec((POOL_WIDTH, POOL_WIDTH)), _const_spec((1, POOL_WIDTH))],
        out_specs=spec,
        compiler_params=_params(1),
        name="pool_mix",
    )(pool_u, w_bd, scale)


def _hgrn_gates(z, log_lb, log_1m_lb):
    z = z * LOG2_E
    log_sig = jnp.minimum(z, 0.0) - jnp.log2(1.0 + jnp.exp2(-jnp.abs(z)))
    b = log_1m_lb + log_sig
    log_f = jnp.maximum(log_lb, b) + jnp.log2(1.0 + jnp.exp2(-jnp.abs(log_lb - b)))
    return log_f, b - z


HG_TILE_CHUNKS = TM // HG_CHUNK
ATTN_ROWS = 128


def _mixer_kernel(q_ref, qn_ref, k_ref, v_ref, hg_ref, lb_ref, nw_ref, ones_ref, attn_ref, hgo_ref,
                  s_ref, gf_ref, hf_ref, gb_ref, hb_ref, of_ref, ob_ref, sf_ref, sb_ref):
    C = HG_CHUNK
    W = HG_WIDTH
    SUB = 8
    t = pl.program_id(1)
    n_tiles = pl.num_programs(1)
    ones_bd = ones_ref[...]
    tpos = lax.broadcasted_iota(jnp.int32, (SUB, W), 0)
    lane_head = lax.broadcasted_iota(jnp.int32, (C, W), 1) // HG_DK

    @pl.when(t == 0)
    def _():
        sf_ref[...] = jnp.zeros((HG_DK, W), F32)
        sb_ref[...] = jnp.zeros((HG_DK, W), F32)

    tile_f = t
    tile_b = jnp.where(t == 0, 0, n_tiles - t)

    def prep(tile, z_col, lb_row, g_ref, h_ref, forward):
        rows = pl.ds(pl.multiple_of(tile * TM, TM), TM)
        pos = lax.broadcasted_iota(jnp.int32, (TM, W), 0) & (C - 1)
        lf, lk = _hgrn_gates(hg_ref[rows, z_col * W:(z_col + 1) * W],
                             lb_ref[lb_row:lb_row + 1, :], lb_ref[lb_row + 1:lb_row + 2, :])
        for s in (1, 2, 4, 8):
            if forward:
                lf = lf + jnp.where(pos >= s, pltpu.roll(lf, s, 0), 0.0)
            else:
                lf = lf + jnp.where(pos < C - s, pltpu.roll(lf, TM - s, 0), 0.0)
        g_ref[rows, :] = lf
        h_ref[rows, :] = lf - lk

    def by_head(x):
        return jnp.concatenate([jnp.where(lane_head == hh, x, 0.0) for hh in range(HG_HEADS)], axis=0)

    def chunk_local(n, g_ref, h_ref, forward):
        rows = pl.ds(pl.multiple_of(n * C, C), C)
        g = g_ref[rows, :]
        h = h_ref[rows, :]
        q = hg_ref[rows, 0:W]
        v = hg_ref[rows, W:2 * W]
        g_end = g[C - 1:C, :] if forward else g[0:1, :]
        q_dec = by_head(q * jnp.exp2(g)).astype(BF16)
        parts, where = [], []
        for s in range(C):
            hs = h[s:s + 1, :]
            for t0 in range(0, C, SUB):
                if (forward and t0 + SUB - 1 < s) or (not forward and t0 > s):
                    continue
                decay = jnp.exp2(g[t0:t0 + SUB, :] - hs)
                if t0 <= s < t0 + SUB:
                    causal = (tpos >= s - t0) if forward else (tpos <= s - t0)
                    decay = jnp.where(causal, decay, 0.0)
                parts.append(q[t0:t0 + SUB, :] * decay)
                where.append((s, t0))
        a = _dot(jnp.concatenate(parts, axis=0).astype(BF16), ones_bd)
        o_tiles = {t0: None for t0 in range(0, C, SUB)}
        for i, (s, t0) in enumerate(where):
            term = a[i * SUB:(i + 1) * SUB, :] * v[s:s + 1, :]
            o_tiles[t0] = term if o_tiles[t0] is None else o_tiles[t0] + term
        o_intra = jnp.concatenate([o_tiles[t0] for t0 in range(0, C, SUB)], axis=0)
        k_dec = by_head(jnp.exp2(g_end - h)).astype(BF16)
        v_rows = jnp.concatenate([v[:, hh * HG_DK:(hh + 1) * HG_DK] for hh in range(HG_HEADS)], axis=0)
        kv = lax.dot_general(v_rows.astype(BF16), k_dec, (((0,), (0,)), ((), ())),
                             preferred_element_type=F32)
        return rows, q_dec, o_intra, jnp.exp2(g_end), kv

    def chunk_carry(st, local, out_ref):
        rows, q_dec, o_intra, a_end, kv = local
        r = _dot_nt(q_dec, st.astype(BF16))
        o_inter = jnp.concatenate([r[hh * C:(hh + 1) * C, :] for hh in range(HG_HEADS)], axis=1)
        out_ref[rows, :] = o_inter + o_intra
        return st * a_end + kv

    def run(n_keys, first_carried):
        prep(tile_f, 2, 0, gf_ref, hf_ref, True)
        prep(tile_b, 3, 2, gb_ref, hb_ref, False)
        st_f = sf_ref[...]
        st_b = sb_ref[...]
        pending = None
        blocks = [(j, slice(r * ATTN_ROWS, (r + 1) * ATTN_ROWS))
                  for j in range(N_HEADS) for r in range(TM // ATTN_ROWS)]
        assert len(blocks) == HG_TILE_CHUNKS
        scores = lambda ref, blk, nk: _dot_nt(ref[blk[0], blk[1], :], k_ref[blk[0] // Q_PER_KV, 0:nk, :])
        s_next = s_ref[...] if first_carried else scores(q_ref, blocks[0], n_keys)
        for i, (j, rows) in enumerate(blocks):
            s = s_next
            if i + 1 < len(blocks):
                s_next = scores(q_ref, blocks[i + 1], n_keys)
            else:
                s_ref[...] = scores(qn_ref, blocks[0], T_ALL)
            m = jnp.max(s, axis=-1, keepdims=True)
            p = jnp.exp2(s - m).astype(BF16)
            acc = _dot(p, v_ref[j // Q_PER_KV, 0:n_keys, :])
            o = acc / pltpu.roll(acc, HEAD_DIM, 1)
            attn_ref[rows, j * HEAD_DIM:(j + 1) * HEAD_DIM] = o[:, 0:HEAD_DIM].astype(BF16)

            local_f = chunk_local(tile_f * HG_TILE_CHUNKS + i, gf_ref, hf_ref, True)
            local_b = chunk_local(tile_b * HG_TILE_CHUNKS + (HG_TILE_CHUNKS - 1 - i), gb_ref, hb_ref, False)
            if pending is not None:
                st_f = chunk_carry(st_f, pending[0], of_ref)
                st_b = chunk_carry(st_b, pending[1], ob_ref)
            pending = (local_f, local_b)
        sf_ref[...] = chunk_carry(st_f, pending[0], of_ref)
        sb_ref[...] = chunk_carry(st_b, pending[1], ob_ref)

    @pl.when(t == 0)
    def _():
        run(CTX_LEN, False)

    @pl.when(t > 0)
    def _():
        run(T_ALL, True)

    @pl.when(t == n_tiles - 1)
    def _():
        def readout(i, carry):
            rows = pl.ds(pl.multiple_of(i * TM, TM), TM)
            o = of_ref[rows, :] + ob_ref[rows, :]
            y = o * lax.rsqrt(_head_mean_sq(o, ones_bd) + NORM_EPS) * nw_ref[...]
            gate = hg_ref[rows, 4 * W:5 * W]
            hgo_ref[rows, :] = (y * (gate * _sigmoid(gate))).astype(BF16)
            return carry

        lax.fori_loop(0, T_ALL // TM, readout, 0)


def _mixer_call(q, k, v, hg5, lb_tab, nw, ones_bd):
    nb = q.shape[0]
    n_tiles = T_ALL // TM
    kv_spec = lambda w: pl.BlockSpec((None, N_KV_HEADS, T_ALL, w), lambda b, t: (b, 0, 0, 0))
    q_spec = lambda nxt: pl.BlockSpec((None, N_HEADS, TM, HEAD_DIM),
                                      lambda b, t: (b, 0, jnp.minimum(t + nxt, n_tiles - 1), 0))
    batch_spec = lambda w: pl.BlockSpec((None, T_ALL, w), lambda b, t: (b, 0, 0))
    seq = lambda: pltpu.VMEM((T_ALL, HG_WIDTH), F32)
    state = lambda: pltpu.VMEM((HG_DK, HG_WIDTH), F32)
    return pl.pallas_call(
        _mixer_kernel,
        out_shape=[jax.ShapeDtypeStruct((nb, T_ALL, ATTN_WIDTH), BF16),
                   jax.ShapeDtypeStruct((nb, T_ALL, HG_WIDTH), BF16)],
        grid=(nb, n_tiles),
        in_specs=[q_spec(0), q_spec(1), kv_spec(HEAD_DIM), kv_spec(2 * HEAD_DIM), batch_spec(5 * HG_WIDTH),
                  _const_spec((4, HG_WIDTH)), _const_spec((1, HG_WIDTH)), _const_spec((HG_WIDTH, HG_WIDTH))],
        out_specs=[_tok_spec(ATTN_WIDTH), batch_spec(HG_WIDTH)],
        scratch_shapes=[pltpu.VMEM((ATTN_ROWS, T_ALL), F32), seq(), seq(), seq(), seq(), seq(), seq(),
                        state(), state()],
        compiler_params=_params(2),
        name="attn_hgrn",
    )(q, q, k, v, hg5, lb_tab, nw, ones_bd)


def _mix_ffn_kernel(*refs):
    (pool_ref, attn_ref, hgo_ref, gate_ref, mod_ref, nw_ref,
     wbp_ref, wba_ref, wbh_ref, wo_ref, w1_ref, w2_ref, o_ref) = refs[-13:]
    y = (gate_ref[:, 0:D_MODEL].astype(F32) * _dot(pool_ref[...], wbp_ref[...])
         + gate_ref[:, D_MODEL:2 * D_MODEL].astype(F32) * _dot(attn_ref[...], wba_ref[...])
         + gate_ref[:, 2 * D_MODEL:3 * D_MODEL].astype(F32) * _dot(hgo_ref[...], wbh_ref[...]))
    x = _stream_tile(refs[:-13]) + mod_ref[2:3, :] * _dot(y.astype(BF16), wo_ref[...])
    h = _rms_mod(x, nw_ref[...], mod_ref[4:5, :], mod_ref[3:4, :]).astype(BF16)
    a = _dot(h, w1_ref[:, 0:D_FF])
    b = _dot(h, w1_ref[:, D_FF:2 * D_FF])
    s = (a * _sigmoid(a) * b).astype(BF16)
    o_ref[...] = x + mod_ref[5:6, :] * _dot(s, w2_ref[...])


def _mix_ffn_call(pool_o, attn_o, hg_o, gates, stream, mods, nw, wbp, wba, wbh, wo, w1, w2, l, latent_only):
    nb = stream[0].shape[0]
    split = len(stream) == 2
    assert not (split and latent_only)
    t_off = 1 if latent_only else 0
    n_rows = SEQ if latent_only else T_ALL
    return pl.pallas_call(
        _mix_ffn_kernel,
        out_shape=jax.ShapeDtypeStruct((nb, n_rows, D_MODEL), F32),
        grid=(nb, n_rows // TM),
        in_specs=(_stream_specs(True) if split else [_tok_spec(D_MODEL, t_off)]) + [
                  _tok_spec(POOL_WIDTH, t_off), _tok_spec(ATTN_WIDTH, t_off), _tok_spec(HG_WIDTH, t_off),
                  _tok_spec(3 * D_MODEL, t_off), _mod_spec(l, t_off),
                  _layer_spec((1, D_MODEL), l),
                  _layer_spec((POOL_WIDTH, D_MODEL), l), _layer_spec((ATTN_WIDTH, D_MODEL), l),
                  _layer_spec((HG_WIDTH, D_MODEL), l), _layer_spec((D_MODEL, D_MODEL), l),
                  _layer_spec((D_MODEL, 2 * D_FF), l), _layer_spec((D_FF, D_MODEL), l)],
        out_specs=_tok_spec(D_MODEL),
        compiler_params=_params(2),
        name="mix_ffn",
    )(*stream, pool_o, attn_o, hg_o, gates, mods, nw, wbp, wba, wbh, wo, w1, w2)


def _rope_tables():
    t = jnp.arange(SEQ, dtype=jnp.int32)
    freqs = ROPE_THETA ** (-jnp.arange(ROPE_HALF, dtype=F32) * (2.0 / (2 * ROPE_HALF)))
    cos_parts, sin_parts = [], []
    for pos in (t // GRID_W, t % GRID_W):
        ang = pos.astype(F32)[:, None] * freqs[None, :]
        cos_parts += [jnp.cos(ang), jnp.cos(ang)]
        sin_parts += [-jnp.sin(ang), jnp.sin(ang)]
    cos = jnp.concatenate(cos_parts, axis=1)
    sin = jnp.concatenate(sin_parts, axis=1)
    cos = jnp.concatenate([jnp.ones((CTX_LEN, HEAD_DIM), F32), cos], axis=0)
    sin = jnp.concatenate([jnp.zeros((CTX_LEN, HEAD_DIM), F32), sin], axis=0)
    return jnp.tile(cos, (1, N_KV_HEADS)), jnp.tile(sin, (1, N_KV_HEADS))


def _block_diag_ones(n):
    i = jnp.arange(n) // HEAD_DIM
    return (i[:, None] == i[None, :]).astype(BF16)


def _pool_block_diag(pool_w_l):
    w = jnp.zeros((POOL_WIDTH, POOL_WIDTH), F32)
    for g in range(POOL_WIDTH // POOL_GROUP):
        sl = slice(g * POOL_GROUP, (g + 1) * POOL_GROUP)
        w = w.at[sl, sl].set(pool_w_l[g])
    return w.astype(BF16)


def kernel(x, c, ctx, c_ctx, w_ada, b_ada, norm1_w, w_in, pool_w, pool_scale, q_norm_w, k_norm_w,
           hg_lb_logits, hg_norm_w, w_branch_pool, w_branch_attn, w_branch_hg, w_out, norm2_w,
           w_ffn_in, w_ffn_out):
    lb_cum = jnp.cumsum(jax.nn.softmax(hg_lb_logits.astype(F32), axis=0), axis=0)
    lb_all = lb_cum - lb_cum[0]
    lb_tab = jnp.stack([jnp.log(lb_all), jnp.log1p(-lb_all)], axis=2) * LOG2_E
    lb_tab = lb_tab.reshape(DEPTH, 4, HG_WIDTH)

    cos_t, sin_t = _rope_tables()
    ones_bd = _block_diag_ones(ONES_WIDTH)

    bf = lambda w: w.astype(BF16)
    w_in, wbp, wba, wbh, wo = bf(w_in), bf(w_branch_pool), bf(w_branch_attn), bf(w_branch_hg), bf(w_out)
    w1, w2 = bf(w_ffn_in), bf(w_ffn_out)
    nw1 = norm1_w.reshape(DEPTH, 1, D_MODEL)
    nw2 = norm2_w.reshape(DEPTH, 1, D_MODEL)
    qw = jnp.tile(q_norm_w, (1, N_HEADS)).reshape(DEPTH, 1, ATTN_WIDTH)
    kw = jnp.tile(k_norm_w, (1, N_KV_HEADS)).reshape(DEPTH, 1, KV_WIDTH)
    hw = jnp.tile(hg_norm_w, (1, HG_HEADS)).reshape(DEPTH, 1, HG_WIDTH)

    cc = jnp.zeros((MOD_ROWS, D_MODEL), F32).at[:BATCH].set(c).at[BATCH].set(c_ctx)
    mods = _ada_call(cc, w_ada, b_ada).reshape(DEPTH, MOD_ROWS, N_MOD, D_MODEL)

    stream = (ctx, x)
    for l in range(DEPTH):
        pool_u, q, k, v, hg5, gates = _inproj_call(stream, mods, nw1, w_in, qw, kw, cos_t, sin_t, ones_bd, l)
        pool_o = _pool_call(pool_u, _pool_block_diag(pool_w[l]), pool_scale[l].reshape(1, POOL_WIDTH))
        attn_o, hg_o = _mixer_call(q, k, v, hg5, lb_tab[l], hw[l], ones_bd)
        stream = (_mix_ffn_call(pool_o, attn_o, hg_o, gates, stream, mods, nw2, wbp, wba, wbh, wo, w1, w2, l,
                                latent_only=(l == DEPTH - 1)),)
    return stream[0]
```

```python
import jax
import jax.numpy as jnp
from jax import lax
from jax.experimental import pallas as pl
from jax.experimental.pallas import tpu as pltpu

D_MODEL = 1024
BATCH = 8
SEQ = 2048
DEPTH = 4
GRID_W = 64
CTX_LEN = 256
T_ALL = CTX_LEN + SEQ
NORM_EPS = 1e-6
N_MOD = 6

POOL_GROUP = 64
POOL_WIDTH = 256
N_HEADS = 8
N_KV_HEADS = 2
HEAD_DIM = 64
Q_PER_KV = N_HEADS // N_KV_HEADS
ATTN_WIDTH = N_HEADS * HEAD_DIM
KV_WIDTH = N_KV_HEADS * HEAD_DIM
ROPE_THETA = 10000.0
ROPE_HALF = HEAD_DIM // 4
LOG2_E = 1.4426950408889634
ONES_WIDTH = 256
HG_HEADS = 4
HG_DK = 64
HG_WIDTH = 256
HG_CHUNK = 16
D_FF = 2816
IN_WIDTH = 5376

C_POOL = 0
C_QKV = 256
C_HG = 1024
C_GATE = 2304

MOD_ROWS = 16
TM = 256
VMEM_LIMIT = 56 * 1024 * 1024

BF16 = jnp.bfloat16
F32 = jnp.float32


def _dot(a, b):
    return jnp.dot(a, b, preferred_element_type=F32)


def _dot_nt(a, b):
    return lax.dot_general(a, b, (((1,), (1,)), ((), ())), preferred_element_type=F32)


def _sigmoid(x):
    return 1.0 / (1.0 + jnp.exp(-x))


def _params(n_grid):
    return pltpu.CompilerParams(dimension_semantics=("arbitrary",) * n_grid,
                                vmem_limit_bytes=VMEM_LIMIT)


def _ada_kernel(c_ref, w_ref, b_ref, o_ref):
    c = c_ref[...]
    act = (c * _sigmoid(c)).astype(BF16)
    o_ref[...] = _dot(act, w_ref[...].astype(BF16)) + b_ref[...]


def _ada_call(cc, w_ada, b_ada):
    tn = 1536
    n = N_MOD * D_MODEL
    return pl.pallas_call(
        _ada_kernel,
        out_shape=jax.ShapeDtypeStruct((DEPTH, MOD_ROWS, n), F32),
        grid=(DEPTH, n // tn),
        in_specs=[
            pl.BlockSpec((MOD_ROWS, D_MODEL), lambda l, j: (0, 0)),
            pl.BlockSpec((None, D_MODEL, tn), lambda l, j: (l, 0, j)),
            pl.BlockSpec((None, 1, tn), lambda l, j: (l, 0, j)),
        ],
        out_specs=pl.BlockSpec((None, MOD_ROWS, tn), lambda l, j: (l, 0, j)),
        compiler_params=_params(2),
        name="ada_mod",
    )(cc, w_ada, b_ada.reshape(DEPTH, 1, n))


def _mod_spec(l, t_off=0):
    return pl.BlockSpec((None, None, N_MOD, D_MODEL),
                        lambda b, t: (l, jnp.where(t + t_off == 0, BATCH, b), 0, 0))


def _const_spec(shape):
    nd = len(shape)
    return pl.BlockSpec(shape, lambda *_: (0,) * nd, pipeline_mode=pl.Buffered(1))


def _layer_spec(shape, l):
    nd = len(shape)
    return pl.BlockSpec((None,) + tuple(shape), lambda *_: (l,) + (0,) * nd,
                        pipeline_mode=pl.Buffered(1))


def _tok_spec(width, t_off=0):
    return pl.BlockSpec((None, TM, width), lambda b, t: (b, t + t_off, 0))


def _rms_mod(x, nw, scale, shift):
    ms = jnp.mean(x * x, axis=-1, keepdims=True)
    y = x * lax.rsqrt(ms + NORM_EPS) * nw
    return y * (1.0 + scale) + shift


def _head_mean_sq(x, ones_bd):
    bw = ones_bd.shape[0]
    if x.shape[1] > bw:
        return jnp.concatenate([_head_mean_sq(x[:, i:i + bw], ones_bd) for i in range(0, x.shape[1], bw)],
                               axis=1)
    x2 = x * x
    hi = x2.astype(BF16)
    lo = (x2 - hi.astype(F32)).astype(BF16)
    return (_dot(hi, ones_bd) + _dot(lo, ones_bd)) * (1.0 / HEAD_DIM)


def _swap_rope_halves(x):
    w = x.shape[1]
    lane = lax.broadcasted_iota(jnp.int32, x.shape, 1)
    up = pltpu.roll(x, w - ROPE_HALF, 1)
    dn = pltpu.roll(x, ROPE_HALF, 1)
    return jnp.where((lane & ROPE_HALF) == 0, up, dn)


def _stream_tile(x_refs):
    if len(x_refs) == 1:
        return x_refs[0][...]
    ctx_ref, lat_ref = x_refs
    return jnp.where(pl.program_id(1) == 0, ctx_ref[...], lat_ref[...])


def _stream_specs(split):
    if not split:
        return [_tok_spec(D_MODEL)]
    return [pl.BlockSpec((None, TM, D_MODEL), lambda b, t: (b, 0, 0)),
            pl.BlockSpec((None, TM, D_MODEL), lambda b, t: (b, jnp.maximum(t - 1, 0), 0))]


def _inproj_kernel(*refs):
    (mod_ref, nw_ref, w_ref, qw_ref, kw_ref, cos_ref, sin_ref, ones_ref,
     pool_ref, q_ref, k_ref, v_ref, hg_ref, gate_ref) = refs[-14:]
    h = _rms_mod(_stream_tile(refs[:-14]), nw_ref[...], mod_ref[1:2, :], mod_ref[0:1, :]).astype(BF16)
    pool_ref[...] = _dot(h, w_ref[:, C_POOL:C_QKV])

    qkv = _dot(h, w_ref[:, C_QKV:C_HG])
    cos = cos_ref[...]
    sin = sin_ref[...]
    q = qkv[:, 0:ATTN_WIDTH]
    q = q * lax.rsqrt(_head_mean_sq(q, ones_ref[...]) + NORM_EPS) * qw_ref[...]
    cos_q = jnp.concatenate([cos] * (ATTN_WIDTH // KV_WIDTH), axis=1)
    sin_q = jnp.concatenate([sin] * (ATTN_WIDTH // KV_WIDTH), axis=1)
    q = (q * cos_q + _swap_rope_halves(q) * sin_q) * (HEAD_DIM ** -0.5 * LOG2_E)
    for hh in range(N_HEADS):
        q_ref[hh] = q[:, hh * HEAD_DIM:(hh + 1) * HEAD_DIM].astype(BF16)
    k = qkv[:, ATTN_WIDTH:ATTN_WIDTH + KV_WIDTH]
    k = k * lax.rsqrt(_head_mean_sq(k, ones_ref[0:KV_WIDTH, 0:KV_WIDTH]) + NORM_EPS) * kw_ref[...]
    k = k * cos + _swap_rope_halves(k) * sin
    v = qkv[:, ATTN_WIDTH + KV_WIDTH:]
    ones = jnp.ones((TM, HEAD_DIM), F32)
    for hh in range(N_KV_HEADS):
        k_ref[hh] = k[:, hh * HEAD_DIM:(hh + 1) * HEAD_DIM].astype(BF16)
        v_ref[hh] = jnp.concatenate([v[:, hh * HEAD_DIM:(hh + 1) * HEAD_DIM], ones], axis=1).astype(BF16)

    hg_ref[...] = _dot(h, w_ref[:, C_HG:C_GATE])
    gate_ref[...] = _sigmoid(_dot(h, w_ref[:, C_GATE:IN_WIDTH])).astype(BF16)


def _inproj_call(stream, mods, nw, w_in, qw, kw, cos_t, sin_t, ones_bd, l):
    nb = stream[0].shape[0]
    head_spec = lambda nh, w: pl.BlockSpec((None, nh, TM, w), lambda b, t: (b, 0, t, 0))
    rope_spec = pl.BlockSpec((TM, KV_WIDTH), lambda b, t: (t, 0))
    out_shape = [
        jax.ShapeDtypeStruct((nb, T_ALL, POOL_WIDTH), F32),
        jax.ShapeDtypeStruct((nb, N_HEADS, T_ALL, HEAD_DIM), BF16),
        jax.ShapeDtypeStruct((nb, N_KV_HEADS, T_ALL, HEAD_DIM), BF16),
        jax.ShapeDtypeStruct((nb, N_KV_HEADS, T_ALL, 2 * HEAD_DIM), BF16),
        jax.ShapeDtypeStruct((nb, T_ALL, C_GATE - C_HG), F32),
        jax.ShapeDtypeStruct((nb, T_ALL, IN_WIDTH - C_GATE), BF16),
    ]
    out_specs = [_tok_spec(POOL_WIDTH), head_spec(N_HEADS, HEAD_DIM), head_spec(N_KV_HEADS, HEAD_DIM),
                 head_spec(N_KV_HEADS, 2 * HEAD_DIM), _tok_spec(C_GATE - C_HG), _tok_spec(IN_WIDTH - C_GATE)]
    return pl.pallas_call(
        _inproj_kernel,
        out_shape=out_shape,
        grid=(nb, T_ALL // TM),
        in_specs=_stream_specs(len(stream) == 2) + [
            _mod_spec(l), _layer_spec((1, D_MODEL), l),
            _layer_spec((D_MODEL, IN_WIDTH), l), _layer_spec((1, ATTN_WIDTH), l),
            _layer_spec((1, KV_WIDTH), l), rope_spec, rope_spec,
            _const_spec((ONES_WIDTH, ONES_WIDTH))],
        out_specs=out_specs,
        compiler_params=_params(2),
        name="in_proj",
    )(*stream, mods, nw, w_in, qw, kw, cos_t, sin_t, ones_bd)


def _pool_segment(u, w_bd, scale):
    n = u.shape[0]
    row = lax.broadcasted_iota(jnp.int32, u.shape, 0)
    grp = lax.broadcasted_iota(jnp.int32, u.shape, 1) // POOL_GROUP

    def prev(x, s):
        return jnp.where(row >= s, pltpu.roll(x, s, 0), 0.0)

    def nxt(x, s):
        return jnp.where(row < n - s, pltpu.roll(x, n - s, 0), 0.0)

    fwd = [u]
    bwd = [prev(u, 1)]
    for j in range(3):
        s = 1 << j
        fwd.append(fwd[j] + nxt(fwd[j], s))
        bwd.append(bwd[j] + prev(bwd[j], s))
    win = fwd[3] + bwd[3]
    for j in (2, 1, 0):
        win = jnp.where(grp == j, fwd[j] + bwd[j], win)
    half = jnp.left_shift(1, grp)
    cnt = jnp.minimum(row + half, n) - jnp.maximum(row - half, 0)
    y = win / cnt.astype(F32) - u
    return (_dot(y.astype(BF16), w_bd) * scale).astype(BF16)


def _pool_kernel(u_ref, w_ref, s_ref, o_ref):
    o_ref[0:CTX_LEN, :] = _pool_segment(u_ref[0:CTX_LEN, :], w_ref[...], s_ref[...])
    o_ref[CTX_LEN:T_ALL, :] = _pool_segment(u_ref[CTX_LEN:T_ALL, :], w_ref[...], s_ref[...])


def _pool_call(pool_u, w_bd, scale):
    nb = pool_u.shape[0]
    spec = pl.BlockSpec((None, T_ALL, POOL_WIDTH), lambda b: (b, 0, 0))
    return pl.pallas_call(
        _pool_kernel,
        out_shape=jax.ShapeDtypeStruct((nb, T_ALL, POOL_WIDTH), BF16),
        grid=(nb,),
        in_specs=[spec, _const_spec((POOL_WIDTH, POOL_WIDTH)), _const_spec((1, POOL_WIDTH))],
        out_specs=spec,
        compiler_params=_params(1),
        name="pool_mix",
    )(pool_u, w_bd, scale)


def _hgrn_gates(z, log_lb, log_1m_lb):
    z = z * LOG2_E
    log_sig = jnp.minimum(z, 0.0) - jnp.log2(1.0 + jnp.exp2(-jnp.abs(z)))
    b = log_1m_lb + log_sig
    log_f = jnp.maximum(log_lb, b) + jnp.log2(1.0 + jnp.exp2(-jnp.abs(log_lb - b)))
    return log_f, b - z


HG_TILE_CHUNKS = TM // HG_CHUNK
ATTN_ROWS = 256


def _mixer_kernel(q_ref, qn_ref, k_ref, v_ref, hg_ref, lb_ref, nw_ref, ones_ref, attn_ref, hgo_ref,
                  s_ref, gf_ref, hf_ref, gb_ref, hb_ref, of_ref, ob_ref, sf_ref, sb_ref):
    C = HG_CHUNK
    W = HG_WIDTH
    SUB = 8
    t = pl.program_id(1)
    n_tiles = pl.num_programs(1)
    ones_bd = ones_ref[...]
    tpos = lax.broadcasted_iota(jnp.int32, (SUB, W), 0)
    lane_head = lax.broadcasted_iota(jnp.int32, (C, W), 1) // HG_DK

    @pl.when(t == 0)
    def _():
        sf_ref[...] = jnp.zeros((HG_DK, W), F32)
        sb_ref[...] = jnp.zeros((HG_DK, W), F32)

    tile_f = t
    tile_b = jnp.where(t == 0, 0, n_tiles - t)

    def prep(tile, z_col, lb_row, g_ref, h_ref, forward):
        rows = pl.ds(pl.multiple_of(tile * TM, TM), TM)
        pos = lax.broadcasted_iota(jnp.int32, (TM, W), 0) & (C - 1)
        lf, lk = _hgrn_gates(hg_ref[rows, z_col * W:(z_col + 1) * W],
                             lb_ref[lb_row:lb_row + 1, :], lb_ref[lb_row + 1:lb_row + 2, :])
        for s in (1, 2, 4, 8):
            if forward:
                lf = lf + jnp.where(pos >= s, pltpu.roll(lf, s, 0), 0.0)
            else:
                lf = lf + jnp.where(pos < C - s, pltpu.roll(lf, TM - s, 0), 0.0)
        g_ref[rows, :] = lf
        h_ref[rows, :] = lf - lk

    def by_head(x):
        return jnp.concatenate([jnp.where(lane_head == hh, x, 0.0) for hh in range(HG_HEADS)], axis=0)

    def chunk_local(n, g_ref, h_ref, forward):
        rows = pl.ds(pl.multiple_of(n * C, C), C)
        g = g_ref[rows, :]
        h = h_ref[rows, :]
        q = hg_ref[rows, 0:W]
        v = hg_ref[rows, W:2 * W]
        g_end = g[C - 1:C, :] if forward else g[0:1, :]
        q_dec = by_head(q * jnp.exp2(g)).astype(BF16)
        parts, where = [], []
        for s in range(C):
            hs = h[s:s + 1, :]
            for t0 in range(0, C, SUB):
                if (forward and t0 + SUB - 1 < s) or (not forward and t0 > s):
                    continue
                decay = jnp.exp2(g[t0:t0 + SUB, :] - hs)
                if t0 <= s < t0 + SUB:
                    causal = (tpos >= s - t0) if forward else (tpos <= s - t0)
                    decay = jnp.where(causal, decay, 0.0)
                parts.append(q[t0:t0 + SUB, :] * decay)
                where.append((s, t0))
        a = _dot(jnp.concatenate(parts, axis=0).astype(BF16), ones_bd)
        o_tiles = {t0: None for t0 in range(0, C, SUB)}
        for i, (s, t0) in enumerate(where):
            term = a[i * SUB:(i + 1) * SUB, :] * v[s:s + 1, :]
            o_tiles[t0] = term if o_tiles[t0] is None else o_tiles[t0] + term
        o_intra = jnp.concatenate([o_tiles[t0] for t0 in range(0, C, SUB)], axis=0)
        k_dec = by_head(jnp.exp2(g_end - h)).astype(BF16)
        v_rows = jnp.concatenate([v[:, hh * HG_DK:(hh + 1) * HG_DK] for hh in range(HG_HEADS)], axis=0)
        kv = lax.dot_general(v_rows.astype(BF16), k_dec, (((0,), (0,)), ((), ())),
                             preferred_element_type=F32)
        return rows, q_dec, o_intra, jnp.exp2(g_end), kv

    def chunk_carry(st, local, out_ref):
        rows, q_dec, o_intra, a_end, kv = local
        r = _dot_nt(q_dec, st.astype(BF16))
        o_inter = jnp.concatenate([r[hh * C:(hh + 1) * C, :] for hh in range(HG_HEADS)], axis=1)
        out_ref[rows, :] = o_inter + o_intra
        return st * a_end + kv

    def run(n_keys, first_carried):
        prep(tile_f, 2, 0, gf_ref, hf_ref, True)
        prep(tile_b, 3, 2, gb_ref, hb_ref, False)
        st_f = sf_ref[...]
        st_b = sb_ref[...]
        pending = None
        blocks = [(j, slice(r * ATTN_ROWS, (r + 1) * ATTN_ROWS))
                  for j in range(N_HEADS) for r in range(TM // ATTN_ROWS)]
        pairs_per_block = HG_TILE_CHUNKS // len(blocks)
        assert pairs_per_block * len(blocks) == HG_TILE_CHUNKS
        scores = lambda ref, blk, nk: _dot_nt(ref[blk[0], blk[1], :], k_ref[blk[0] // Q_PER_KV, 0:nk, :])
        s_next = s_ref[...] if first_carried else scores(q_ref, blocks[0], n_keys)
        for i, (j, rows) in enumerate(blocks):
            s = s_next
            if i + 1 < len(blocks):
                s_next = scores(q_ref, blocks[i + 1], n_keys)
            else:
                s_ref[...] = scores(qn_ref, blocks[0], T_ALL)
            m = jnp.max(s, axis=-1, keepdims=True)
            p = jnp.exp2(s - m).astype(BF16)
            acc = _dot(p, v_ref[j // Q_PER_KV, 0:n_keys, :])
            o = acc / pltpu.roll(acc, HEAD_DIM, 1)
            attn_ref[rows, j * HEAD_DIM:(j + 1) * HEAD_DIM] = o[:, 0:HEAD_DIM].astype(BF16)

            for u in range(pairs_per_block):
                c = i * pairs_per_block + u
                local_f = chunk_local(tile_f * HG_TILE_CHUNKS + c, gf_ref, hf_ref, True)
                local_b = chunk_local(tile_b * HG_TILE_CHUNKS + (HG_TILE_CHUNKS - 1 - c), gb_ref, hb_ref, False)
                if pending is not None:
                    st_f = chunk_carry(st_f, pending[0], of_ref)
                    st_b = chunk_carry(st_b, pending[1], ob_ref)
                pending = (local_f, local_b)
        sf_ref[...] = chunk_carry(st_f, pending[0], of_ref)
        sb_ref[...] = chunk_carry(st_b, pending[1], ob_ref)

    @pl.when(t == 0)
    def _():
        run(CTX_LEN, False)

    @pl.when(t > 0)
    def _():
        run(T_ALL, True)

    @pl.when(t == n_tiles - 1)
    def _():
        def readout(i, carry):
            rows = pl.ds(pl.multiple_of(i * TM, TM), TM)
            o = of_ref[rows, :] + ob_ref[rows, :]
            y = o * lax.rsqrt(_head_mean_sq(o, ones_bd) + NORM_EPS) * nw_ref[...]
            gate = hg_ref[rows, 4 * W:5 * W]
            hgo_ref[rows, :] = (y * (gate * _sigmoid(gate))).astype(BF16)
            return carry

        lax.fori_loop(0, T_ALL // TM, readout, 0)


def _mixer_call(q, k, v, hg5, lb_tab, nw, ones_bd):
    nb = q.shape[0]
    n_tiles = T_ALL // TM
    kv_spec = lambda w: pl.BlockSpec((None, N_KV_HEADS, T_ALL, w), lambda b, t: (b, 0, 0, 0))
    q_spec = lambda nxt: pl.BlockSpec((None, N_HEADS, TM, HEAD_DIM),
                                      lambda b, t: (b, 0, jnp.minimum(t + nxt, n_tiles - 1), 0))
    batch_spec = lambda w: pl.BlockSpec((None, T_ALL, w), lambda b, t: (b, 0, 0))
    seq = lambda: pltpu.VMEM((T_ALL, HG_WIDTH), F32)
    state = lambda: pltpu.VMEM((HG_DK, HG_WIDTH), F32)
    return pl.pallas_call(
        _mixer_kernel,
        out_shape=[jax.ShapeDtypeStruct((nb, T_ALL, ATTN_WIDTH), BF16),
                   jax.ShapeDtypeStruct((nb, T_ALL, HG_WIDTH), BF16)],
        grid=(nb, n_tiles),
        in_specs=[q_spec(0), q_spec(1), kv_spec(HEAD_DIM), kv_spec(2 * HEAD_DIM), batch_spec(5 * HG_WIDTH),
                  _const_spec((4, HG_WIDTH)), _const_spec((1, HG_WIDTH)), _const_spec((HG_WIDTH, HG_WIDTH))],
        out_specs=[_tok_spec(ATTN_WIDTH), batch_spec(HG_WIDTH)],
        scratch_shapes=[pltpu.VMEM((ATTN_ROWS, T_ALL), F32), seq(), seq(), seq(), seq(), seq(), seq(),
                        state(), state()],
        compiler_params=_params(2),
        name="attn_hgrn",
    )(q, q, k, v, hg5, lb_tab, nw, ones_bd)


def _mix_ffn_kernel(*refs):
    (pool_ref, attn_ref, hgo_ref, gate_ref, mod_ref, nw_ref,
     wbp_ref, wba_ref, wbh_ref, wo_ref, w1_ref, w2_ref, o_ref) = refs[-13:]
    y = (gate_ref[:, 0:D_MODEL].astype(F32) * _dot(pool_ref[...], wbp_ref[...])
         + gate_ref[:, D_MODEL:2 * D_MODEL].astype(F32) * _dot(attn_ref[...], wba_ref[...])
         + gate_ref[:, 2 * D_MODEL:3 * D_MODEL].astype(F32) * _dot(hgo_ref[...], wbh_ref[...]))
    x = _stream_tile(refs[:-13]) + mod_ref[2:3, :] * _dot(y.astype(BF16), wo_ref[...])
    h = _rms_mod(x, nw_ref[...], mod_ref[4:5, :], mod_ref[3:4, :]).astype(BF16)
    a = _dot(h, w1_ref[:, 0:D_FF])
    b = _dot(h, w1_ref[:, D_FF:2 * D_FF])
    s = (a * _sigmoid(a) * b).astype(BF16)
    o_ref[...] = x + mod_ref[5:6, :] * _dot(s, w2_ref[...])


def _mix_ffn_call(pool_o, attn_o, hg_o, gates, stream, mods, nw, wbp, wba, wbh, wo, w1, w2, l, latent_only):
    nb = stream[0].shape[0]
    split = len(stream) == 2
    assert not (split and latent_only)
    t_off = 1 if latent_only else 0
    n_rows = SEQ if latent_only else T_ALL
    return pl.pallas_call(
        _mix_ffn_kernel,
        out_shape=jax.ShapeDtypeStruct((nb, n_rows, D_MODEL), F32),
        grid=(nb, n_rows // TM),
        in_specs=(_stream_specs(True) if split else [_tok_spec(D_MODEL, t_off)]) + [
                  _tok_spec(POOL_WIDTH, t_off), _tok_spec(ATTN_WIDTH, t_off), _tok_spec(HG_WIDTH, t_off),
                  _tok_spec(3 * D_MODEL, t_off), _mod_spec(l, t_off),
                  _layer_spec((1, D_MODEL), l),
                  _layer_spec((POOL_WIDTH, D_MODEL), l), _layer_spec((ATTN_WIDTH, D_MODEL), l),
                  _layer_spec((HG_WIDTH, D_MODEL), l), _layer_spec((D_MODEL, D_MODEL), l),
                  _layer_spec((D_MODEL, 2 * D_FF), l), _layer_spec((D_FF, D_MODEL), l)],
        out_specs=_tok_spec(D_MODEL),
        compiler_params=_params(2),
        name="mix_ffn",
    )(*stream, pool_o, attn_o, hg_o, gates, mods, nw, wbp, wba, wbh, wo, w1, w2)


def _rope_tables():
    t = jnp.arange(SEQ, dtype=jnp.int32)
    freqs = ROPE_THETA ** (-jnp.arange(ROPE_HALF, dtype=F32) * (2.0 / (2 * ROPE_HALF)))
    cos_parts, sin_parts = [], []
    for pos in (t // GRID_W, t % GRID_W):
        ang = pos.astype(F32)[:, None] * freqs[None, :]
        cos_parts += [jnp.cos(ang), jnp.cos(ang)]
        sin_parts += [-jnp.sin(ang), jnp.sin(ang)]
    cos = jnp.concatenate(cos_parts, axis=1)
    sin = jnp.concatenate(sin_parts, axis=1)
    cos = jnp.concatenate([jnp.ones((CTX_LEN, HEAD_DIM), F32), cos], axis=0)
    sin = jnp.concatenate([jnp.zeros((CTX_LEN, HEAD_DIM), F32), sin], axis=0)
    return jnp.tile(cos, (1, N_KV_HEADS)), jnp.tile(sin, (1, N_KV_HEADS))


def _block_diag_ones(n):
    i = jnp.arange(n) // HEAD_DIM
    return (i[:, None] == i[None, :]).astype(BF16)


def _pool_block_diag(pool_w_l):
    w = jnp.zeros((POOL_WIDTH, POOL_WIDTH), F32)
    for g in range(POOL_WIDTH // POOL_GROUP):
        sl = slice(g * POOL_GROUP, (g + 1) * POOL_GROUP)
        w = w.at[sl, sl].set(pool_w_l[g])
    return w.astype(BF16)


def kernel(x, c, ctx, c_ctx, w_ada, b_ada, norm1_w, w_in, pool_w, pool_scale, q_norm_w, k_norm_w,
           hg_lb_logits, hg_norm_w, w_branch_pool, w_branch_attn, w_branch_hg, w_out, norm2_w,
           w_ffn_in, w_ffn_out):
    lb_cum = jnp.cumsum(jax.nn.softmax(hg_lb_logits.astype(F32), axis=0), axis=0)
    lb_all = lb_cum - lb_cum[0]
    lb_tab = jnp.stack([jnp.log(lb_all), jnp.log1p(-lb_all)], axis=2) * LOG2_E
    lb_tab = lb_tab.reshape(DEPTH, 4, HG_WIDTH)

    cos_t, sin_t = _rope_tables()
    ones_bd = _block_diag_ones(ONES_WIDTH)

    bf = lambda w: w.astype(BF16)
    w_in, wbp, wba, wbh, wo = bf(w_in), bf(w_branch_pool), bf(w_branch_attn), bf(w_branch_hg), bf(w_out)
    w1, w2 = bf(w_ffn_in), bf(w_ffn_out)
    nw1 = norm1_w.reshape(DEPTH, 1, D_MODEL)
    nw2 = norm2_w.reshape(DEPTH, 1, D_MODEL)
    qw = jnp.tile(q_norm_w, (1, N_HEADS)).reshape(DEPTH, 1, ATTN_WIDTH)
    kw = jnp.tile(k_norm_w, (1, N_KV_HEADS)).reshape(DEPTH, 1, KV_WIDTH)
    hw = jnp.tile(hg_norm_w, (1, HG_HEADS)).reshape(DEPTH, 1, HG_WIDTH)

    cc = jnp.zeros((MOD_ROWS, D_MODEL), F32).at[:BATCH].set(c).at[BATCH].set(c_ctx)
    mods = _ada_call(cc, w_ada, b_ada).reshape(DEPTH, MOD_ROWS, N_MOD, D_MODEL)

    stream = (ctx, x)
    for l in range(DEPTH):
        pool_u, q, k, v, hg5, gates = _inproj_call(stream, mods, nw1, w_in, qw, kw, cos_t, sin_t, ones_bd, l)
        pool_o = _pool_call(pool_u, _pool_block_diag(pool_w[l]), pool_scale[l].reshape(1, POOL_WIDTH))
        attn_o, hg_o = _mixer_call(q, k, v, hg5, lb_tab[l], hw[l], ones_bd)
        stream = (_mix_ffn_call(pool_o, attn_o, hg_o, gates, stream, mods, nw2, wbp, wba, wbh, wo, w1, w2, l,
                                latent_only=(l == DEPTH - 1)),)
    return stream[0]
```

```python
import jax
import jax.numpy as jnp
from jax import lax
from jax.experimental import pallas as pl
from jax.experimental.pallas import tpu as pltpu

D_MODEL = 1024
BATCH = 8
SEQ = 2048
DEPTH = 4
GRID_W = 64
CTX_LEN = 256
T_ALL = CTX_LEN + SEQ
NORM_EPS = 1e-6
N_MOD = 6

POOL_GROUP = 64
POOL_WIDTH = 256
N_HEADS = 8
N_KV_HEADS = 2
HEAD_DIM = 64
Q_PER_KV = N_HEADS // N_KV_HEADS
ATTN_WIDTH = N_HEADS * HEAD_DIM
KV_WIDTH = N_KV_HEADS * HEAD_DIM
ROPE_THETA = 10000.0
ROPE_HALF = HEAD_DIM // 4
LOG2_E = 1.4426950408889634
ONES_WIDTH = 256
HG_HEADS = 4
HG_DK = 64
HG_WIDTH = 256
HG_CHUNK = 16
D_FF = 2816
IN_WIDTH = 5376

C_POOL = 0
C_QKV = 256
C_HG = 1024
C_GATE = 2304

MOD_ROWS = 16
TM = 256
VMEM_LIMIT = 56 * 1024 * 1024

BF16 = jnp.bfloat16
F32 = jnp.float32


def _dot(a, b):
    return jnp.dot(a, b, preferred_element_type=F32)


def _dot_nt(a, b):
    return lax.dot_general(a, b, (((1,), (1,)), ((), ())), preferred_element_type=F32)


def _sigmoid(x):
    return 1.0 / (1.0 + jnp.exp(-x))


def _params(n_grid):
    return pltpu.CompilerParams(dimension_semantics=("arbitrary",) * n_grid,
                                vmem_limit_bytes=VMEM_LIMIT)


def _ada_kernel(c_ref, w_ref, b_ref, o_ref):
    c = c_ref[...]
    act = (c * _sigmoid(c)).astype(BF16)
    o_ref[...] = _dot(act, w_ref[...].astype(BF16)) + b_ref[...]


def _ada_call(cc, w_ada, b_ada):
    tn = 1536
    n = N_MOD * D_MODEL
    return pl.pallas_call(
        _ada_kernel,
        out_shape=jax.ShapeDtypeStruct((DEPTH, MOD_ROWS, n), F32),
        grid=(DEPTH, n // tn),
        in_specs=[
            pl.BlockSpec((MOD_ROWS, D_MODEL), lambda l, j: (0, 0)),
            pl.BlockSpec((None, D_MODEL, tn), lambda l, j: (l, 0, j)),
            pl.BlockSpec((None, 1, tn), lambda l, j: (l, 0, j)),
        ],
        out_specs=pl.BlockSpec((None, MOD_ROWS, tn), lambda l, j: (l, 0, j)),
        compiler_params=_params(2),
        name="ada_mod",
    )(cc, w_ada, b_ada.reshape(DEPTH, 1, n))


def _mod_spec(l, t_off=0):
    return pl.BlockSpec((None, None, N_MOD, D_MODEL),
                        lambda b, t: (l, jnp.where(t + t_off == 0, BATCH, b), 0, 0))


def _const_spec(shape):
    nd = len(shape)
    return pl.BlockSpec(shape, lambda *_: (0,) * nd, pipeline_mode=pl.Buffered(1))


def _layer_spec(shape, l):
    nd = len(shape)
    return pl.BlockSpec((None,) + tuple(shape), lambda *_: (l,) + (0,) * nd,
                        pipeline_mode=pl.Buffered(1))


def _tok_spec(width, t_off=0):
    return pl.BlockSpec((None, TM, width), lambda b, t: (b, t + t_off, 0))


def _rms_mod(x, nw, scale, shift):
    ms = jnp.mean(x * x, axis=-1, keepdims=True)
    y = x * lax.rsqrt(ms + NORM_EPS) * nw
    return y * (1.0 + scale) + shift


def _head_mean_sq(x, ones_bd):
    bw = ones_bd.shape[0]
    if x.shape[1] > bw:
        return jnp.concatenate([_head_mean_sq(x[:, i:i + bw], ones_bd) for i in range(0, x.shape[1], bw)],
                               axis=1)
    x2 = x * x
    hi = x2.astype(BF16)
    lo = (x2 - hi.astype(F32)).astype(BF16)
    return (_dot(hi, ones_bd) + _dot(lo, ones_bd)) * (1.0 / HEAD_DIM)


def _swap_rope_halves(x):
    w = x.shape[1]
    lane = lax.broadcasted_iota(jnp.int32, x.shape, 1)
    up = pltpu.roll(x, w - ROPE_HALF, 1)
    dn = pltpu.roll(x, ROPE_HALF, 1)
    return jnp.where((lane & ROPE_HALF) == 0, up, dn)


def _stream_tile(x_refs):
    if len(x_refs) == 1:
        return x_refs[0][...]
    ctx_ref, lat_ref = x_refs
    return jnp.where(pl.program_id(1) == 0, ctx_ref[...], lat_ref[...])


def _stream_specs(split):
    if not split:
        return [_tok_spec(D_MODEL)]
    return [pl.BlockSpec((None, TM, D_MODEL), lambda b, t: (b, 0, 0)),
            pl.BlockSpec((None, TM, D_MODEL), lambda b, t: (b, jnp.maximum(t - 1, 0), 0))]


def _inproj_kernel(*refs):
    (mod_ref, nw_ref, w_ref, qw_ref, kw_ref, cos_ref, sin_ref, ones_ref,
     pool_ref, q_ref, k_ref, v_ref, hg_ref, gate_ref) = refs[-14:]
    h = _rms_mod(_stream_tile(refs[:-14]), nw_ref[...], mod_ref[1:2, :], mod_ref[0:1, :]).astype(BF16)
    pool_ref[...] = _dot(h, w_ref[:, C_POOL:C_QKV])

    qkv = _dot(h, w_ref[:, C_QKV:C_HG])
    cos = cos_ref[...]
    sin = sin_ref[...]
    q = qkv[:, 0:ATTN_WIDTH]
    q = q * lax.rsqrt(_head_mean_sq(q, ones_ref[...]) + NORM_EPS) * qw_ref[...]
    cos_q = jnp.concatenate([cos] * (ATTN_WIDTH // KV_WIDTH), axis=1)
    sin_q = jnp.concatenate([sin] * (ATTN_WIDTH // KV_WIDTH), axis=1)
    q = (q * cos_q + _swap_rope_halves(q) * sin_q) * (HEAD_DIM ** -0.5 * LOG2_E)
    for hh in range(N_HEADS):
        q_ref[hh] = q[:, hh * HEAD_DIM:(hh + 1) * HEAD_DIM].astype(BF16)
    k = qkv[:, ATTN_WIDTH:ATTN_WIDTH + KV_WIDTH]
    k = k * lax.rsqrt(_head_mean_sq(k, ones_ref[0:KV_WIDTH, 0:KV_WIDTH]) + NORM_EPS) * kw_ref[...]
    k = k * cos + _swap_rope_halves(k) * sin
    v = qkv[:, ATTN_WIDTH + KV_WIDTH:]
    ones = jnp.ones((TM, HEAD_DIM), F32)
    for hh in range(N_KV_HEADS):
        k_ref[hh] = k[:, hh * HEAD_DIM:(hh + 1) * HEAD_DIM].astype(BF16)
        v_ref[hh] = jnp.concatenate([v[:, hh * HEAD_DIM:(hh + 1) * HEAD_DIM], ones], axis=1).astype(BF16)

    hg_ref[...] = _dot(h, w_ref[:, C_HG:C_GATE])
    gate_ref[...] = _sigmoid(_dot(h, w_ref[:, C_GATE:IN_WIDTH])).astype(BF16)


def _inproj_call(stream, mods, nw, w_in, qw, kw, cos_t, sin_t, ones_bd, l):
    nb = stream[0].shape[0]
    head_spec = lambda nh, w: pl.BlockSpec((None, nh, TM, w), lambda b, t: (b, 0, t, 0))
    rope_spec = pl.BlockSpec((TM, KV_WIDTH), lambda b, t: (t, 0))
    out_shape = [
        jax.ShapeDtypeStruct((nb, T_ALL, POOL_WIDTH), F32),
        jax.ShapeDtypeStruct((nb, N_HEADS, T_ALL, HEAD_DIM), BF16),
        jax.ShapeDtypeStruct((nb, N_KV_HEADS, T_ALL, HEAD_DIM), BF16),
        jax.ShapeDtypeStruct((nb, N_KV_HEADS, T_ALL, 2 * HEAD_DIM), BF16),
        jax.ShapeDtypeStruct((nb, T_ALL, C_GATE - C_HG), F32),
        jax.ShapeDtypeStruct((nb, T_ALL, IN_WIDTH - C_GATE), BF16),
    ]
    out_specs = [_tok_spec(POOL_WIDTH), head_spec(N_HEADS, HEAD_DIM), head_spec(N_KV_HEADS, HEAD_DIM),
                 head_spec(N_KV_HEADS, 2 * HEAD_DIM), _tok_spec(C_GATE - C_HG), _tok_spec(IN_WIDTH - C_GATE)]
    return pl.pallas_call(
        _inproj_kernel,
        out_shape=out_shape,
        grid=(nb, T_ALL // TM),
        in_specs=_stream_specs(len(stream) == 2) + [
            _mod_spec(l), _layer_spec((1, D_MODEL), l),
            _layer_spec((D_MODEL, IN_WIDTH), l), _layer_spec((1, ATTN_WIDTH), l),
            _layer_spec((1, KV_WIDTH), l), rope_spec, rope_spec,
            _const_spec((ONES_WIDTH, ONES_WIDTH))],
        out_specs=out_specs,
        compiler_params=_params(2),
        name="in_proj",
    )(*stream, mods, nw, w_in, qw, kw, cos_t, sin_t, ones_bd)


def _pool_segment(u, w_bd, scale):
    n = u.shape[0]
    row = lax.broadcasted_iota(jnp.int32, u.shape, 0)
    grp = lax.broadcasted_iota(jnp.int32, u.shape, 1) // POOL_GROUP

    def prev(x, s):
        return jnp.where(row >= s, pltpu.roll(x, s, 0), 0.0)

    def nxt(x, s):
        return jnp.where(row < n - s, pltpu.roll(x, n - s, 0), 0.0)

    fwd = [u]
    bwd = [prev(u, 1)]
    for j in range(3):
        s = 1 << j
        fwd.append(fwd[j] + nxt(fwd[j], s))
        bwd.append(bwd[j] + prev(bwd[j], s))
    win = fwd[3] + bwd[3]
    for j in (2, 1, 0):
        win = jnp.where(grp == j, fwd[j] + bwd[j], win)
    half = jnp.left_shift(1, grp)
    cnt = jnp.minimum(row + half, n) - jnp.maximum(row - half, 0)
    y = win / cnt.astype(F32) - u
    return (_dot(y.astype(BF16), w_bd) * scale).astype(BF16)


def _pool_kernel(u_ref, w_ref, s_ref, o_ref):
    o_ref[0:CTX_LEN, :] = _pool_segment(u_ref[0:CTX_LEN, :], w_ref[...], s_ref[...])
    o_ref[CTX_LEN:T_ALL, :] = _pool_segment(u_ref[CTX_LEN:T_ALL, :], w_ref[...], s_ref[...])


def _pool_call(pool_u, w_bd, scale):
    nb = pool_u.shape[0]
    spec = pl.BlockSpec((None, T_ALL, POOL_WIDTH), lambda b: (b, 0, 0))
    return pl.pallas_call(
        _pool_kernel,
        out_shape=jax.ShapeDtypeStruct((nb, T_ALL, POOL_WIDTH), BF16),
        grid=(nb,),
        in_specs=[spec, _const_spec((POOL_WIDTH, POOL_WIDTH)), _const_spec((1, POOL_WIDTH))],
        out_specs=spec,
        compiler_params=_params(1),
        name="pool_mix",
    )(pool_u, w_bd, scale)


def _hgrn_gates(z, log_lb, log_1m_lb):
    z = z * LOG2_E
    log_sig = jnp.minimum(z, 0.0) - jnp.log2(1.0 + jnp.exp2(-jnp.abs(z)))
    b = log_1m_lb + log_sig
    log_f = jnp.maximum(log_lb, b) + jnp.log2(1.0 + jnp.exp2(-jnp.abs(log_lb - b)))
    return log_f, b - z


HG_TILE_CHUNKS = TM // HG_CHUNK
ATTN_ROWS = 256


def _mixer_kernel(q_ref, qn_ref, k_ref, v_ref, hg_ref, lb_ref, nw_ref, ones_ref, attn_ref, hgo_ref,
                  s_ref, gf_ref, hf_ref, gb_ref, hb_ref, of_ref, ob_ref, sf_ref, sb_ref):
    C = HG_CHUNK
    W = HG_WIDTH
    SUB = 8
    t = pl.program_id(1)
    n_tiles = pl.num_programs(1)
    ones_bd = ones_ref[...]
    tpos = lax.broadcasted_iota(jnp.int32, (SUB, W), 0)
    lane_head = lax.broadcasted_iota(jnp.int32, (C, W), 1) // HG_DK

    @pl.when(t == 0)
    def _():
        sf_ref[...] = jnp.zeros((HG_DK, W), F32)
        sb_ref[...] = jnp.zeros((HG_DK, W), F32)

    tile_f = t
    tile_b = jnp.where(t == 0, 0, n_tiles - t)

    def prep(tile, z_col, lb_row, g_ref, h_ref, forward):
        rows = pl.ds(pl.multiple_of(tile * TM, TM), TM)
        pos = lax.broadcasted_iota(jnp.int32, (TM, W), 0) & (C - 1)
        lf, lk = _hgrn_gates(hg_ref[rows, z_col * W:(z_col + 1) * W],
                             lb_ref[lb_row:lb_row + 1, :], lb_ref[lb_row + 1:lb_row + 2, :])
        for s in (1, 2, 4, 8):
            if forward:
                lf = lf + jnp.where(pos >= s, pltpu.roll(lf, s, 0), 0.0)
            else:
                lf = lf + jnp.where(pos < C - s, pltpu.roll(lf, TM - s, 0), 0.0)
        g_ref[rows, :] = lf
        h_ref[rows, :] = lf - lk

    def by_head(x):
        return jnp.concatenate([jnp.where(lane_head == hh, x, 0.0) for hh in range(HG_HEADS)], axis=0)

    def chunk_local(n, g_ref, h_ref, forward):
        rows = pl.ds(pl.multiple_of(n * C, C), C)
        g = g_ref[rows, :]
        h = h_ref[rows, :]
        q = hg_ref[rows, 0:W]
        v = hg_ref[rows, W:2 * W]
        g_end = g[C - 1:C, :] if forward else g[0:1, :]
        q_dec = by_head(q * jnp.exp2(g)).astype(BF16)
        parts, where = [], []
        for s in range(C):
            hs = h[s:s + 1, :]
            for t0 in range(0, C, SUB):
                if (forward and t0 + SUB - 1 < s) or (not forward and t0 > s):
                    continue
                decay = jnp.exp2(g[t0:t0 + SUB, :] - hs)
                if t0 <= s < t0 + SUB:
                    causal = (tpos >= s - t0) if forward else (tpos <= s - t0)
                    decay = jnp.where(causal, decay, 0.0)
                parts.append(q[t0:t0 + SUB, :] * decay)
                where.append((s, t0))
        a = _dot(jnp.concatenate(parts, axis=0).astype(BF16), ones_bd)
        o_tiles = {t0: None for t0 in range(0, C, SUB)}
        for i, (s, t0) in enumerate(where):
            term = a[i * SUB:(i + 1) * SUB, :] * v[s:s + 1, :]
            o_tiles[t0] = term if o_tiles[t0] is None else o_tiles[t0] + term
        o_intra = jnp.concatenate([o_tiles[t0] for t0 in range(0, C, SUB)], axis=0)
        k_dec = by_head(jnp.exp2(g_end - h)).astype(BF16)
        v_rows = jnp.concatenate([v[:, hh * HG_DK:(hh + 1) * HG_DK] for hh in range(HG_HEADS)], axis=0)
        kv = lax.dot_general(v_rows.astype(BF16), k_dec, (((0,), (0,)), ((), ())),
                             preferred_element_type=F32)
        return rows, q_dec, o_intra, jnp.exp2(g_end), kv

    def chunk_carry(st, local, out_ref):
        rows, q_dec, o_intra, a_end, kv = local
        r = _dot_nt(q_dec, st.astype(BF16))
        o_inter = jnp.concatenate([r[hh * C:(hh + 1) * C, :] for hh in range(HG_HEADS)], axis=1)
        out_ref[rows, :] = o_inter + o_intra
        return st * a_end + kv

    def run(n_keys, first_carried):
        prep(tile_f, 2, 0, gf_ref, hf_ref, True)
        prep(tile_b, 3, 2, gb_ref, hb_ref, False)
        st_f = sf_ref[...]
        st_b = sb_ref[...]
        pending = None
        blocks = [(j, slice(r * ATTN_ROWS, (r + 1) * ATTN_ROWS))
                  for j in range(N_HEADS) for r in range(TM // ATTN_ROWS)]
        pairs_per_block = HG_TILE_CHUNKS // len(blocks)
        assert pairs_per_block * len(blocks) == HG_TILE_CHUNKS
        scores = lambda ref, blk, nk: _dot_nt(ref[blk[0], blk[1], :], k_ref[blk[0] // Q_PER_KV, 0:nk, :])
        s_next = s_ref[...] if first_carried else scores(q_ref, blocks[0], n_keys)
        for i, (j, rows) in enumerate(blocks):
            s = s_next
            if i + 1 < len(blocks):
                s_next = scores(q_ref, blocks[i + 1], n_keys)
            else:
                s_ref[...] = scores(qn_ref, blocks[0], T_ALL)

            for u in range(pairs_per_block):
                c = i * pairs_per_block + u
                local_f = chunk_local(tile_f * HG_TILE_CHUNKS + c, gf_ref, hf_ref, True)
                local_b = chunk_local(tile_b * HG_TILE_CHUNKS + (HG_TILE_CHUNKS - 1 - c), gb_ref, hb_ref, False)
                if pending is not None:
                    st_f = chunk_carry(st_f, pending[0], of_ref)
                    st_b = chunk_carry(st_b, pending[1], ob_ref)
                pending = (local_f, local_b)

            m = jnp.max(s, axis=-1, keepdims=True)
            p = jnp.exp2(s - m).astype(BF16)
            acc = _dot(p, v_ref[j // Q_PER_KV, 0:n_keys, :])
            o = acc / pltpu.roll(acc, HEAD_DIM, 1)
            attn_ref[rows, j * HEAD_DIM:(j + 1) * HEAD_DIM] = o[:, 0:HEAD_DIM].astype(BF16)
        sf_ref[...] = chunk_carry(st_f, pending[0], of_ref)
        sb_ref[...] = chunk_carry(st_b, pending[1], ob_ref)

    @pl.when(t == 0)
    def _():
        run(CTX_LEN, False)

    @pl.when(t > 0)
    def _():
        run(T_ALL, True)

    @pl.when(t == n_tiles - 1)
    def _():
        def readout(i, carry):
            rows = pl.ds(pl.multiple_of(i * TM, TM), TM)
            o = of_ref[rows, :] + ob_ref[rows, :]
            y = o * lax.rsqrt(_head_mean_sq(o, ones_bd) + NORM_EPS) * nw_ref[...]
            gate = hg_ref[rows, 4 * W:5 * W]
            hgo_ref[rows, :] = (y * (gate * _sigmoid(gate))).astype(BF16)
            return carry

        lax.fori_loop(0, T_ALL // TM, readout, 0)


def _mixer_call(q, k, v, hg5, lb_tab, nw, ones_bd):
    nb = q.shape[0]
    n_tiles = T_ALL // TM
    kv_spec = lambda w: pl.BlockSpec((None, N_KV_HEADS, T_ALL, w), lambda b, t: (b, 0, 0, 0))
    q_spec = lambda nxt: pl.BlockSpec((None, N_HEADS, TM, HEAD_DIM),
                                      lambda b, t: (b, 0, jnp.minimum(t + nxt, n_tiles - 1), 0))
    batch_spec = lambda w: pl.BlockSpec((None, T_ALL, w), lambda b, t: (b, 0, 0))
    seq = lambda: pltpu.VMEM((T_ALL, HG_WIDTH), F32)
    state = lambda: pltpu.VMEM((HG_DK, HG_WIDTH), F32)
    return pl.pallas_call(
        _mixer_kernel,
        out_shape=[jax.ShapeDtypeStruct((nb, T_ALL, ATTN_WIDTH), BF16),
                   jax.ShapeDtypeStruct((nb, T_ALL, HG_WIDTH), BF16)],
        grid=(nb, n_tiles),
        in_specs=[q_spec(0), q_spec(1), kv_spec(HEAD_DIM), kv_spec(2 * HEAD_DIM), batch_spec(5 * HG_WIDTH),
                  _const_spec((4, HG_WIDTH)), _const_spec((1, HG_WIDTH)), _const_spec((HG_WIDTH, HG_WIDTH))],
        out_specs=[_tok_spec(ATTN_WIDTH), batch_spec(HG_WIDTH)],
        scratch_shapes=[pltpu.VMEM((ATTN_ROWS, T_ALL), F32), seq(), seq(), seq(), seq(), seq(), seq(),
                        state(), state()],
        compiler_params=_params(2),
        name="attn_hgrn",
    )(q, q, k, v, hg5, lb_tab, nw, ones_bd)


def _mix_ffn_kernel(*refs):
    (pool_ref, attn_ref, hgo_ref, gate_ref, mod_ref, nw_ref,
     wbp_ref, wba_ref, wbh_ref, wo_ref, w1_ref, w2_ref, o_ref) = refs[-13:]
    y = (gate_ref[:, 0:D_MODEL].astype(F32) * _dot(pool_ref[...], wbp_ref[...])
         + gate_ref[:, D_MODEL:2 * D_MODEL].astype(F32) * _dot(attn_ref[...], wba_ref[...])
         + gate_ref[:, 2 * D_MODEL:3 * D_MODEL].astype(F32) * _dot(hgo_ref[...], wbh_ref[...]))
    x = _stream_tile(refs[:-13]) + mod_ref[2:3, :] * _dot(y.astype(BF16), wo_ref[...])
    h = _rms_mod(x, nw_ref[...], mod_ref[4:5, :], mod_ref[3:4, :]).astype(BF16)
    a = _dot(h, w1_ref[:, 0:D_FF])
    b = _dot(h, w1_ref[:, D_FF:2 * D_FF])
    s = (a * _sigmoid(a) * b).astype(BF16)
    o_ref[...] = x + mod_ref[5:6, :] * _dot(s, w2_ref[...])


def _mix_ffn_call(pool_o, attn_o, hg_o, gates, stream, mods, nw, wbp, wba, wbh, wo, w1, w2, l, latent_only):
    nb = stream[0].shape[0]
    split = len(stream) == 2
    assert not (split and latent_only)
    t_off = 1 if latent_only else 0
    n_rows = SEQ if latent_only else T_ALL
    return pl.pallas_call(
        _mix_ffn_kernel,
        out_shape=jax.ShapeDtypeStruct((nb, n_rows, D_MODEL), F32),
        grid=(nb, n_rows // TM),
        in_specs=(_stream_specs(True) if split else [_tok_spec(D_MODEL, t_off)]) + [
                  _tok_spec(POOL_WIDTH, t_off), _tok_spec(ATTN_WIDTH, t_off), _tok_spec(HG_WIDTH, t_off),
                  _tok_spec(3 * D_MODEL, t_off), _mod_spec(l, t_off),
                  _layer_spec((1, D_MODEL), l),
                  _layer_spec((POOL_WIDTH, D_MODEL), l), _layer_spec((ATTN_WIDTH, D_MODEL), l),
                  _layer_spec((HG_WIDTH, D_MODEL), l), _layer_spec((D_MODEL, D_MODEL), l),
                  _layer_spec((D_MODEL, 2 * D_FF), l), _layer_spec((D_FF, D_MODEL), l)],
        out_specs=_tok_spec(D_MODEL),
        compiler_params=_params(2),
        name="mix_ffn",
    )(*stream, pool_o, attn_o, hg_o, gates, mods, nw, wbp, wba, wbh, wo, w1, w2)


def _rope_tables():
    t = jnp.arange(SEQ, dtype=jnp.int32)
    freqs = ROPE_THETA ** (-jnp.arange(ROPE_HALF, dtype=F32) * (2.0 / (2 * ROPE_HALF)))
    cos_parts, sin_parts = [], []
    for pos in (t // GRID_W, t % GRID_W):
        ang = pos.astype(F32)[:, None] * freqs[None, :]
        cos_parts += [jnp.cos(ang), jnp.cos(ang)]
        sin_parts += [-jnp.sin(ang), jnp.sin(ang)]
    cos = jnp.concatenate(cos_parts, axis=1)
    sin = jnp.concatenate(sin_parts, axis=1)
    cos = jnp.concatenate([jnp.ones((CTX_LEN, HEAD_DIM), F32), cos], axis=0)
    sin = jnp.concatenate([jnp.zeros((CTX_LEN, HEAD_DIM), F32), sin], axis=0)
    return jnp.tile(cos, (1, N_KV_HEADS)), jnp.tile(sin, (1, N_KV_HEADS))


def _block_diag_ones(n):
    i = jnp.arange(n) // HEAD_DIM
    return (i[:, None] == i[None, :]).astype(BF16)


def _pool_block_diag(pool_w_l):
    w = jnp.zeros((POOL_WIDTH, POOL_WIDTH), F32)
    for g in range(POOL_WIDTH // POOL_GROUP):
        sl = slice(g * POOL_GROUP, (g + 1) * POOL_GROUP)
        w = w.at[sl, sl].set(pool_w_l[g])
    return w.astype(BF16)


def kernel(x, c, ctx, c_ctx, w_ada, b_ada, norm1_w, w_in, pool_w, pool_scale, q_norm_w, k_norm_w,
           hg_lb_logits, hg_norm_w, w_branch_pool, w_branch_attn, w_branch_hg, w_out, norm2_w,
           w_ffn_in, w_ffn_out):
    lb_cum = jnp.cumsum(jax.nn.softmax(hg_lb_logits.astype(F32), axis=0), axis=0)
    lb_all = lb_cum - lb_cum[0]
    lb_tab = jnp.stack([jnp.log(lb_all), jnp.log1p(-lb_all)], axis=2) * LOG2_E
    lb_tab = lb_tab.reshape(DEPTH, 4, HG_WIDTH)

    cos_t, sin_t = _rope_tables()
    ones_bd = _block_diag_ones(ONES_WIDTH)

    bf = lambda w: w.astype(BF16)
    w_in, wbp, wba, wbh, wo = bf(w_in), bf(w_branch_pool), bf(w_branch_attn), bf(w_branch_hg), bf(w_out)
    w1, w2 = bf(w_ffn_in), bf(w_ffn_out)
    nw1 = norm1_w.reshape(DEPTH, 1, D_MODEL)
    nw2 = norm2_w.reshape(DEPTH, 1, D_MODEL)
    qw = jnp.tile(q_norm_w, (1, N_HEADS)).reshape(DEPTH, 1, ATTN_WIDTH)
    kw = jnp.tile(k_norm_w, (1, N_KV_HEADS)).reshape(DEPTH, 1, KV_WIDTH)
    hw = jnp.tile(hg_norm_w, (1, HG_HEADS)).reshape(DEPTH, 1, HG_WIDTH)

    cc = jnp.zeros((MOD_ROWS, D_MODEL), F32).at[:BATCH].set(c).at[BATCH].set(c_ctx)
    mods = _ada_call(cc, w_ada, b_ada).reshape(DEPTH, MOD_ROWS, N_MOD, D_MODEL)

    stream = (ctx, x)
    for l in range(DEPTH):
        pool_u, q, k, v, hg5, gates = _inproj_call(stream, mods, nw1, w_in, qw, kw, cos_t, sin_t, ones_bd, l)
        pool_o = _pool_call(pool_u, _pool_block_diag(pool_w[l]), pool_scale[l].reshape(1, POOL_WIDTH))
        attn_o, hg_o = _mixer_call(q, k, v, hg5, lb_tab[l], hw[l], ones_bd)
        stream = (_mix_ffn_call(pool_o, attn_o, hg_o, gates, stream, mods, nw2, wbp, wba, wbh, wo, w1, w2, l,
                                latent_only=(l == DEPTH - 1)),)
    return stream[0]
```
